```python
import jax, jax.numpy as jnp
from jax import lax
import numpy as np

D_MODEL = 4096
BATCH = 1
SEQ = 8192
DEPTH = 1
DEC_BATCH = 32
DEC_SEQ = 8
PAST_LEN = 8192
PAGE_SIZE = 128

HEAD_DIM = 128
N_HEADS = D_MODEL // (2 * HEAD_DIM)
ATT_W = N_HEADS * HEAD_DIM
CHUNK = 128
GM_GROUPS = 16
GM_GROUP_W = D_MODEL // (2 * GM_GROUPS)
GM_W = GM_GROUPS * GM_GROUP_W
D_FF = 11008
CONV_W = 3
Q_BLOCK = 128
EPS = 1e-6
NEG = -1e30
N_IN = 3 * ATT_W + N_HEADS + 2 * GM_W + 2 * D_MODEL
SPLITS = tuple(int(s) for s in np.cumsum([ATT_W, ATT_W, ATT_W, N_HEADS, GM_W, GM_W, D_MODEL]))

kernel_name = 'fox_gmlp_gated_convffn_step'


def _rms(x, g):
    xf = x.astype(jnp.float32)
    y = xf * lax.rsqrt(jnp.mean(xf * xf, axis=-1, keepdims=True) + EPS)
    return (y * g.astype(jnp.float32)).astype(x.dtype)


def _project(h, w_in, b_f, qg, kg):
    B, T = h.shape[:2]
    p = jnp.einsum('btd,dn->btn', h, w_in)
    q, k, v, f, u, vg, ga, gb = jnp.split(p, SPLITS, axis=-1)
    q = _rms(q.reshape(B, T, N_HEADS, HEAD_DIM), qg)
    k = _rms(k.reshape(B, T, N_HEADS, HEAD_DIM), kg)
    v = v.reshape(B, T, N_HEADS, HEAD_DIM)
    logf = jax.nn.log_sigmoid(f.astype(jnp.float32) + b_f.astype(jnp.float32))
    return q, k, v, logf, u, vg, ga, gb


def _fox_prompt(q, k, v, logf):
    B, S = q.shape[:2]
    scale = HEAD_DIM ** -0.5
    c = jnp.cumsum(logf, axis=1)
    c_keys = jnp.transpose(c, (0, 2, 1))
    pos = jnp.arange(S)

    def block(i):
        q0 = i * Q_BLOCK
        qb = lax.dynamic_slice_in_dim(q, q0, Q_BLOCK, axis=1)
        cb = jnp.transpose(lax.dynamic_slice_in_dim(c, q0, Q_BLOCK, axis=1), (0, 2, 1))
        s = jnp.einsum('bthd,bshd->bhts', qb, k).astype(jnp.float32) * scale
        s = s + (cb[..., :, None] - c_keys[..., None, :])
        mask = (q0 + jnp.arange(Q_BLOCK))[:, None] >= pos[None, :]
        s = jnp.where(mask, s, NEG)
        p = jax.nn.softmax(s, axis=-1)
        return jnp.einsum('bhts,bshd->bthd', p.astype(v.dtype), v)

    o = lax.map(block, jnp.arange(S // Q_BLOCK))
    return jnp.moveaxis(o, 0, 1).reshape(B, S, ATT_W)


def _fox_sample(q, k, v, logf, cache_k, cache_v, cache_logf, page_table, layer):
    Bd, T = q.shape[:2]
    n_pages = page_table.shape[1]
    scale = HEAD_DIM ** -0.5
    logf_past = cache_logf[layer, page_table].astype(jnp.float32)
    c_past = jnp.cumsum(logf_past.reshape(Bd, n_pages * PAGE_SIZE, N_HEADS), axis=1)
    c_new = c_past[:, -1:] + jnp.cumsum(logf, axis=1)
    cn = jnp.transpose(c_new, (0, 2, 1))

    s = jnp.einsum('bthd,bshd->bhts', q, k).astype(jnp.float32) * scale
    s = s + (cn[..., :, None] - cn[..., None, :])
    causal = jnp.arange(T)[:, None] >= jnp.arange(T)[None, :]
    s = jnp.where(causal, s, NEG)
    m = jnp.max(s, axis=-1)
    p = jnp.exp(s - m[..., None])
    l = jnp.sum(p, axis=-1)
    acc = jnp.einsum('bhts,bshd->bhtd', p, v.astype(jnp.float32))

    def step(carry, xs):
        m, l, acc = carry
        pages, cp = xs
        kp = cache_k[layer, pages]
        vp = cache_v[layer, pages]
        s = jnp.einsum('bthd,bshd->bhts', q, kp.astype(q.dtype)).astype(jnp.float32) * scale
        s = s + (cn[..., :, None] - jnp.transpose(cp, (0, 2, 1))[..., None, :])
        m_new = jnp.maximum(m, jnp.max(s, axis=-1))
        alpha = jnp.exp(m - m_new)
        p = jnp.exp(s - m_new[..., None])
        l = l * alpha + jnp.sum(p, axis=-1)
        acc = acc * alpha[..., None] + jnp.einsum('bhts,bshd->bhtd', p, vp.astype(jnp.float32))
        return (m_new, l, acc), None

    xs = (jnp.transpose(page_table), jnp.moveaxis(c_past.reshape(Bd, n_pages, PAGE_SIZE, N_HEADS), 1, 0))
    (m, l, acc), _ = lax.scan(step, (m, l, acc), xs)
    o = acc / l[..., None]
    return jnp.transpose(o, (0, 2, 1, 3)).reshape(Bd, T, ATT_W).astype(v.dtype)


def _gmlp(u, vg, gm_g, w_s, b_s):
    B, T = u.shape[:2]
    L = min(T, CHUNK)
    u = jax.nn.gelu(u)
    vn = _rms(jax.nn.gelu(vg), gm_g)
    w = jnp.tril(w_s[:, :L, :L]).astype(vn.dtype)
    b = jnp.transpose(b_s[:, :L])[None, None, :, :, None].astype(vn.dtype)
    vr = vn.reshape(B, T // L, L, GM_GROUPS, GM_GROUP_W)
    mixed = jnp.einsum('gts,bnsgc->bntgc', w, vr) + b
    return u * mixed.reshape(B, T, GM_W), vn


def _merge(ya, yb, ga, gb, w_oa, w_ob, w_out):
    m = (jax.nn.sigmoid(ga) * jnp.einsum('btn,nd->btd', ya, w_oa)
         + jax.nn.sigmoid(gb) * jnp.einsum('btn,nd->btd', yb, w_ob))
    return jnp.einsum('btd,de->bte', m, w_out)


def _conv_ffn(h, prev, w_up, w_gate, conv_w, conv_b, w_down):
    T = h.shape[1]
    a = jnp.einsum('btd,df->btf', h, w_up)
    g = jnp.einsum('btd,df->btf', h, w_gate)
    ext = jnp.concatenate([prev.astype(a.dtype), a], axis=1)
    c = conv_b + sum(conv_w[j] * ext[:, j:j + T] for j in range(CONV_W))
    y = jnp.einsum('btf,fd->btd', jax.nn.silu(c) * g, w_down)
    return y, ext[:, -(CONV_W - 1):]


def setup_inputs(seed: int = 0) -> dict:
    key = jax.random.key(seed)
    ks = jax.random.split(key, 32)
    n_pages = PAST_LEN // PAGE_SIZE
    n_used = DEC_BATCH * n_pages
    n_pool = (n_used * 5) // 4
    nrm = jax.random.normal
    f32 = jnp.float32
    page_table = jax.random.permutation(ks[0], n_pool)[:n_used].reshape(DEC_BATCH, n_pages).astype(jnp.int32)
    return {
        'x_prompt': nrm(ks[1], (BATCH, SEQ, D_MODEL), f32),
        'x_sample': nrm(ks[2], (DEC_BATCH, DEC_SEQ, D_MODEL), f32),
        'cache_k': nrm(ks[3], (DEPTH, n_pool, PAGE_SIZE, N_HEADS, HEAD_DIM), f32),
        'cache_v': nrm(ks[4], (DEPTH, n_pool, PAGE_SIZE, N_HEADS, HEAD_DIM), f32),
        'cache_logf': jax.nn.log_sigmoid(3.0 + nrm(ks[5], (DEPTH, n_pool, PAGE_SIZE, N_HEADS), f32)),
        'state_conv': nrm(ks[6], (DEPTH, DEC_BATCH, CONV_W - 1, D_FF), f32),
        'page_table': page_table,
        'norm1_g': 1.0 + 0.05 * nrm(ks[7], (DEPTH, D_MODEL), f32),
        'w_in': nrm(ks[8], (DEPTH, D_MODEL, N_IN), f32) * D_MODEL ** -0.5,
        'b_f': 3.0 + 0.5 * nrm(ks[9], (DEPTH, N_HEADS), f32),
        'q_norm_g': 1.0 + 0.05 * nrm(ks[10], (DEPTH, HEAD_DIM), f32),
        'k_norm_g': 1.0 + 0.05 * nrm(ks[11], (DEPTH, HEAD_DIM), f32),
        'gm_norm_g': 1.0 + 0.05 * nrm(ks[12], (DEPTH, GM_W), f32),
        'w_s': nrm(ks[13], (DEPTH, GM_GROUPS, CHUNK, CHUNK), f32) * CHUNK ** -0.5,
        'b_s': 1.0 + 0.02 * nrm(ks[14], (DEPTH, GM_GROUPS, CHUNK), f32),
        'w_oa': nrm(ks[15], (DEPTH, ATT_W, D_MODEL), f32) * ATT_W ** -0.5,
        'w_ob': nrm(ks[16], (DEPTH, GM_W, D_MODEL), f32) * GM_W ** -0.5,
        'w_out': nrm(ks[17], (DEPTH, D_MODEL, D_MODEL), f32) * D_MODEL ** -0.5,
        'norm2_g': 1.0 + 0.05 * nrm(ks[18], (DEPTH, D_MODEL), f32),
        'w_up': nrm(ks[19], (DEPTH, D_MODEL, D_FF), f32) * D_MODEL ** -0.5,
        'w_gate': nrm(ks[20], (DEPTH, D_MODEL, D_FF), f32) * D_MODEL ** -0.5,
        'conv_w': nrm(ks[21], (DEPTH, CONV_W, D_FF), f32) * CONV_W ** -0.5,
        'conv_b': 0.02 * nrm(ks[22], (DEPTH, D_FF), f32),
        'w_down': nrm(ks[23], (DEPTH, D_FF, D_MODEL), f32) * D_FF ** -0.5,
    }


def reference(x_prompt, x_sample, cache_k, cache_v, cache_logf, state_conv, page_table,
              norm1_g, w_in, b_f, q_norm_g, k_norm_g, gm_norm_g, w_s, b_s, w_oa, w_ob, w_out,
              norm2_g, w_up, w_gate, conv_w, conv_b, w_down):
    xp, xs = x_prompt, x_sample
    kp_l, vp_l, lfp_l, cvp_l = [], [], [], []
    ks_l, vs_l, lfs_l, gvs_l, cvs_l = [], [], [], [], []
    for l in range(DEPTH):
        q, k, v, lf, u, vg, ga, gb = _project(_rms(xp, norm1_g[l]), w_in[l], b_f[l], q_norm_g[l], k_norm_g[l])
        ya = _fox_prompt(q, k, v, lf)
        yb, _ = _gmlp(u, vg, gm_norm_g[l], w_s[l], b_s[l])
        xp = xp + _merge(ya, yb, ga, gb, w_oa[l], w_ob[l], w_out[l])
        zero_prev = jnp.zeros((xp.shape[0], CONV_W - 1, D_FF), xp.dtype)
        yf, conv_p = _conv_ffn(_rms(xp, norm2_g[l]), zero_prev, w_up[l], w_gate[l], conv_w[l], conv_b[l], w_down[l])
        xp = xp + yf
        kp_l.append(k); vp_l.append(v); lfp_l.append(lf); cvp_l.append(conv_p)

        q, k, v, lf, u, vg, ga, gb = _project(_rms(xs, norm1_g[l]), w_in[l], b_f[l], q_norm_g[l], k_norm_g[l])
        ya = _fox_sample(q, k, v, lf, cache_k, cache_v, cache_logf, page_table, l)
        yb, vn = _gmlp(u, vg, gm_norm_g[l], w_s[l], b_s[l])
        xs = xs + _merge(ya, yb, ga, gb, w_oa[l], w_ob[l], w_out[l])
        yf, conv_s = _conv_ffn(_rms(xs, norm2_g[l]), state_conv[l], w_up[l], w_gate[l], conv_w[l], conv_b[l], w_down[l])
        xs = xs + yf
        ks_l.append(k); vs_l.append(v); lfs_l.append(lf); gvs_l.append(vn); cvs_l.append(conv_s)

    return (xp, xs,
            jnp.stack(kp_l), jnp.stack(vp_l), jnp.stack(lfp_l), jnp.stack(cvp_l),
            jnp.stack(ks_l), jnp.stack(vs_l), jnp.stack(lfs_l), jnp.stack(gvs_l), jnp.stack(cvs_l))
```

```python
import functools

import jax
import jax.numpy as jnp
import numpy as np
from jax import lax
from jax.experimental import pallas as pl
from jax.experimental.pallas import tpu as pltpu

F32 = jnp.float32
BF16 = jnp.bfloat16

LANES = 128
SUBLANES = 8
VMEM_LIMIT = 56 * 1024 * 1024

HEAD_DIM = 128
N_HEADS = 16
ATT_W = N_HEADS * HEAD_DIM
GM_GROUPS = 16
GM_GROUP_W = 128
GM_W = GM_GROUPS * GM_GROUP_W
CHUNK = 128
CONV_W = 3
EPS = 1e-6
NEG = -1e30
SCALE = HEAD_DIM ** -0.5

PROJ_TN = 512
PAGES_PER_STEP = 4


def _cparams(n_axes):
    return pltpu.CompilerParams(dimension_semantics=("arbitrary",) * n_axes, vmem_limit_bytes=VMEM_LIMIT)


def _dot(a, b):
    return jnp.dot(a, b, preferred_element_type=F32)


def _dot_nt(a, b):
    return lax.dot_general(a, b, (((1,), (1,)), ((), ())), preferred_element_type=F32)


def _rms_rows(x, g):
    return x * lax.rsqrt(jnp.mean(x * x, axis=-1, keepdims=True) + EPS) * g


def _sigmoid(x):
    return 1.0 / (1.0 + jnp.exp(-x))


def _gelu_tanh(x):
    return 0.5 * x * (1.0 + jnp.tanh(np.sqrt(2.0 / np.pi).astype(np.float32) * (x + 0.044715 * (x * x * x))))


def _log_sigmoid(x):
    return jnp.minimum(x, 0.0) - jnp.log(1.0 + jnp.exp(-jnp.abs(x)))


_SEC = {"q": (0, 4), "k": (4, 4), "v": (8, 4), "u": (12, 4), "vg": (16, 4), "ga": (20, 8), "gb": (28, 8)}


def _inproj_body(x_ref, g1_ref, w_ref, wf_ref, bf_ref, qg_ref, kg_ref,
                 q_ref, kf_ref, kb_ref, vf_ref, vb_ref, gu_ref, gv_ref, sa_ref, sb_ref, lf_ref, h_scr):
    j = pl.program_id(1)

    @pl.when(j == 0)
    def _():
        hb = _rms_rows(x_ref[...], g1_ref[...]).astype(BF16)
        h_scr[...] = hb
        lf_ref[...] = _log_sigmoid(_dot(hb, wf_ref[...]) + bf_ref[...])

    acc = _dot(h_scr[...], w_ref[...])

    def in_sec(name):
        lo, n = _SEC[name]
        return jnp.logical_and(j >= lo, j < lo + n)

    def head_norm(g):
        return [_rms_rows(acc[:, h * HEAD_DIM:(h + 1) * HEAD_DIM], g) for h in range(PROJ_TN // HEAD_DIM)]

    @pl.when(in_sec("q"))
    def _():
        for h, y in enumerate(head_norm(qg_ref[...])):
            q_ref[:, h * HEAD_DIM:(h + 1) * HEAD_DIM] = y.astype(BF16)

    @pl.when(in_sec("k"))
    def _():
        for h, y in enumerate(head_norm(kg_ref[...])):
            kf_ref[:, h * HEAD_DIM:(h + 1) * HEAD_DIM] = y
            kb_ref[:, h * HEAD_DIM:(h + 1) * HEAD_DIM] = y.astype(BF16)

    @pl.when(in_sec("v"))
    def _():
        vf_ref[...] = acc
        vb_ref[...] = acc.astype(BF16)

    @pl.when(in_sec("u"))
    def _():
        gu_ref[...] = _gelu_tanh(acc).astype(BF16)

    @pl.when(in_sec("vg"))
    def _():
        gv_ref[...] = _gelu_tanh(acc)

    @pl.when(in_sec("ga"))
    def _():
        sa_ref[...] = _sigmoid(acc).astype(BF16)

    @pl.when(in_sec("gb"))
    def _():
        sb_ref[...] = _sigmoid(acc).astype(BF16)


def _inproj(x, g1, wa, wf, bfp, qg, kg, tm):
    m, d = x.shape
    nj = wa.shape[1] // PROJ_TN

    def sec(name):
        lo, n = _SEC[name]
        return pl.BlockSpec((tm, PROJ_TN), lambda i, j: (i, jnp.clip(j - lo, 0, n - 1)))

    def shp(name, dt):
        return jax.ShapeDtypeStruct((m, _SEC[name][1] * PROJ_TN), dt)

    const = lambda i, j: (0, 0)
    return pl.pallas_call(
        _inproj_body,
        grid=(m // tm, nj),
        in_specs=[pl.BlockSpec((tm, d), lambda i, j: (i, 0)),
                  pl.BlockSpec((1, d), const),
                  pl.BlockSpec((d, PROJ_TN), lambda i, j: (0, j)),
                  pl.BlockSpec((d, LANES), const),
                  pl.BlockSpec((1, LANES), const),
                  pl.BlockSpec((1, HEAD_DIM), const),
                  pl.BlockSpec((1, HEAD_DIM), const)],
        out_specs=[sec("q"), sec("k"), sec("k"), sec("v"), sec("v"), sec("u"), sec("vg"), sec("ga"), sec("gb"),
                   pl.BlockSpec((tm, LANES), lambda i, j: (i, 0))],
        out_shape=[shp("q", BF16), shp("k", F32), shp("k", BF16), shp("v", F32), shp("v", BF16),
                   shp("u", BF16), shp("vg", F32), shp("ga", BF16), shp("gb", BF16),
                   jax.ShapeDtypeStruct((m, LANES), F32)],
        scratch_shapes=[pltpu.VMEM((tm, d), BF16)],
        compiler_params=_cparams(2),
        name="inproj",
    )(x, g1, wa, wf, bfp, qg, kg)


def _lane_cumsum(x, shifts, lane):
    for sh in shifts:
        x = x + jnp.where(lane >= sh, pltpu.roll(x, sh, axis=1), 0.0)
    return x


def _negcumsum_body(lf_ref, o_ref):
    nblk = lf_ref.shape[0] // LANES
    lane = lax.broadcasted_iota(jnp.int32, (N_HEADS, LANES), 1)

    def body(b, carry):
        off = pl.multiple_of(b * LANES, LANES)
        xt = jnp.transpose(lf_ref[pl.ds(off, LANES), :])[:N_HEADS, :]
        c = _lane_cumsum(xt, (1, 2, 4, 8, 16, 32, 64), lane) + carry
        o_ref[:, pl.ds(off, LANES)] = -c
        return c[:, LANES - 1:LANES]

    lax.fori_loop(0, nblk, body, jnp.zeros((N_HEADS, 1), F32))


def _negcumsum(lf):
    s = lf.shape[0]
    return pl.pallas_call(
        _negcumsum_body,
        out_shape=jax.ShapeDtypeStruct((N_HEADS, s), F32),
        compiler_params=pltpu.CompilerParams(vmem_limit_bytes=VMEM_LIMIT),
        name="negcumsum",
    )(lf)


def _fox_prompt_body(qi_ref, ki_ref, q_ref, k_ref, v_ref, nc_ref, o_ref, m_scr, l_scr, acc_scr, *, tq, tk):
    p = pl.program_id(1)
    qi = qi_ref[p]
    ki = ki_ref[p]

    @pl.when(ki == 0)
    def _():
        m_scr[...] = jnp.full(m_scr.shape, NEG, F32)
        l_scr[...] = jnp.zeros(l_scr.shape, F32)
        acc_scr[...] = jnp.zeros(acc_scr.shape, F32)

    z = _dot_nt(q_ref[...], k_ref[...]) * SCALE + nc_ref[...]
    row = qi * tq + lax.broadcasted_iota(jnp.int32, (tq, tk), 0)
    col = ki * tk + lax.broadcasted_iota(jnp.int32, (tq, tk), 1)
    z = jnp.where(row >= col, z, NEG)
    m_old = m_scr[...]
    m_new = jnp.maximum(m_old, jnp.max(z, axis=-1, keepdims=True))
    alpha = jnp.exp(m_old - m_new)
    pr = jnp.exp(z - m_new)
    l_scr[...] = alpha * l_scr[...] + jnp.sum(pr, axis=-1, keepdims=True)
    acc_scr[...] = alpha * acc_scr[...] + _dot(pr.astype(BF16), v_ref[...])
    m_scr[...] = m_new

    @pl.when(ki == (qi + 1) * (tq // tk) - 1)
    def _():
        o_ref[...] = (acc_scr[...] / l_scr[...]).astype(o_ref.dtype)


def _fox_prompt(qb, kb, vb, negc, tq=1024, tk=512):
    s = qb.shape[0]
    pairs = [(qi, ki) for qi in range(s // tq) for ki in range((qi + 1) * (tq // tk))]
    qi_arr = jnp.asarray([p[0] for p in pairs], jnp.int32)
    ki_arr = jnp.asarray([p[1] for p in pairs], jnp.int32)
    grid_spec = pltpu.PrefetchScalarGridSpec(
        num_scalar_prefetch=2,
        grid=(N_HEADS, len(pairs)),
        in_specs=[pl.BlockSpec((tq, HEAD_DIM), lambda h, p, qi, ki: (qi[p], h)),
                  pl.BlockSpec((tk, HEAD_DIM), lambda h, p, qi, ki: (ki[p], h)),
                  pl.BlockSpec((tk, HEAD_DIM), lambda h, p, qi, ki: (ki[p], h)),
                  pl.BlockSpec((None, 1, tk), lambda h, p, qi, ki: (h, 0, ki[p]))],
        out_specs=pl.BlockSpec((tq, HEAD_DIM), lambda h, p, qi, ki: (qi[p], h)),
        scratch_shapes=[pltpu.VMEM((tq, 1), F32), pltpu.VMEM((tq, 1), F32), pltpu.VMEM((tq, HEAD_DIM), F32)],
    )
    return pl.pallas_call(
        functools.partial(_fox_prompt_body, tq=tq, tk=tk),
        grid_spec=grid_spec,
        out_shape=jax.ShapeDtypeStruct((s, ATT_W), BF16),
        compiler_params=_cparams(2),
        name="fox_prompt",
    )(qi_arr, ki_arr, qb, kb, vb, negc.reshape(N_HEADS, 1, s))


def _page_negcumsum(lf, run, lane, rowj):
    x = _lane_cumsum(lf, (16, 32, 64), lane)
    t = jnp.where(lane >= LANES - N_HEADS, x, 0.0)
    for sh in (16, 32, 64):
        t = t + pltpu.roll(t, sh, axis=1)
    e = t
    for sh in (1, 2, 4, 8):
        e = e + jnp.where(rowj >= sh, pltpu.roll(e, sh, axis=0), 0.0)
    c = x + (e - t) + run
    return -c, run + e[N_HEADS - 1:N_HEADS, :]


def _fox_sample_body(pt_ref, q_ref, *refs, npg):
    del pt_ref
    k_refs, v_refs, lf_refs = refs[:npg], refs[npg:2 * npg], refs[2 * npg:3 * npg]
    kn_ref, vn_ref, lfn_ref, o_ref, m_scr, l_scr, acc_scr, run_scr = refs[3 * npg:]
    g = pl.program_id(1)
    rows = N_HEADS * SUBLANES
    tiles = k_refs[0].shape[0] // LANES

    @pl.when(g == 0)
    def _():
        m_scr[...] = jnp.full(m_scr.shape, NEG, F32)
        l_scr[...] = jnp.zeros(l_scr.shape, F32)
        acc_scr[...] = jnp.zeros(acc_scr.shape, F32)
        run_scr[...] = jnp.zeros(run_scr.shape, F32)

    lane = lax.broadcasted_iota(jnp.int32, (rows, LANES), 1)
    row = lax.broadcasted_iota(jnp.int32, (rows, LANES), 0)
    same_head = (lane & (N_HEADS - 1)) == (row >> 3)
    lane16 = lax.broadcasted_iota(jnp.int32, (N_HEADS, LANES), 1)
    row16 = lax.broadcasted_iota(jnp.int32, (N_HEADS, LANES), 0)
    q = q_ref[...]

    def attend(zs, vb, m, l, acc):
        zmax = zs[0]
        for z in zs[1:]:
            zmax = jnp.maximum(zmax, z)
        m_new = jnp.maximum(m, jnp.max(zmax, axis=-1, keepdims=True))
        alpha = jnp.exp(m - m_new)
        ps = [jnp.exp(z - m_new) for z in zs]
        psum = ps[0]
        for pp in ps[1:]:
            psum = psum + pp
        l = alpha * l + jnp.sum(psum, axis=-1, keepdims=True)
        pm = ps[0] if len(ps) == 1 else jnp.concatenate(ps, axis=1)
        acc = alpha * acc + _dot(pm.astype(BF16), vb)
        return m_new, l, acc

    m, l, acc, run = m_scr[...], l_scr[...], acc_scr[...], run_scr[...]
    for p in range(npg):
        gt = _dot_nt(q, k_refs[p][...].astype(BF16))
        negc, run = _page_negcumsum(lf_refs[p][...], run, lane16, row16)
        zs = [jnp.where(same_head, gt[:, t * LANES:(t + 1) * LANES] * SCALE + negc[t:t + 1, :], NEG)
              for t in range(tiles)]
        m, l, acc = attend(zs, v_refs[p][...].astype(BF16), m, l, acc)
    m_scr[...], l_scr[...], acc_scr[...], run_scr[...] = m, l, acc, run

    @pl.when(g == pl.num_programs(1) - 1)
    def _():
        cn = _lane_cumsum(lfn_ref[...], (16, 32, 64), lane16[:1]) + run
        zn = _dot_nt(q, kn_ref[...]) * SCALE - cn
        keep = jnp.logical_and(same_head, (lane >> 4) <= (row & (SUBLANES - 1)))
        _, l2, acc2 = attend([jnp.where(keep, zn, NEG)], vn_ref[...], m, l, acc)
        o_ref[...] = acc2 / l2


def _fox_sample(page_table, qcat, ck, cv, clf, kn, vn, lfn):
    nb = qcat.shape[0]
    n_pages = page_table.shape[1]
    npg = PAGES_PER_STEP
    rows = N_HEADS * SUBLANES
    prow = ck.shape[1]

    def page(p):
        return lambda b, g, pt: (pt[b * n_pages + g * npg + p], 0, 0)

    per_b = lambda b, g, pt: (b, 0, 0)
    in_specs = ([pl.BlockSpec((None, rows, HEAD_DIM), per_b)]
                + [pl.BlockSpec((None, prow, HEAD_DIM), page(p)) for p in range(npg)]
                + [pl.BlockSpec((None, prow, HEAD_DIM), page(p)) for p in range(npg)]
                + [pl.BlockSpec((None, N_HEADS, LANES), page(p)) for p in range(npg)]
                + [pl.BlockSpec((None, rows, HEAD_DIM), per_b),
                   pl.BlockSpec((None, rows, HEAD_DIM), per_b),
                   pl.BlockSpec((None, 1, LANES), per_b)])
    grid_spec = pltpu.PrefetchScalarGridSpec(
        num_scalar_prefetch=1,
        grid=(nb, n_pages // npg),
        in_specs=in_specs,
        out_specs=pl.BlockSpec((None, rows, HEAD_DIM), per_b),
        scratch_shapes=[pltpu.VMEM((rows, 1), F32), pltpu.VMEM((rows, 1), F32),
                        pltpu.VMEM((rows, HEAD_DIM), F32), pltpu.VMEM((1, LANES), F32)],
    )
    return pl.pallas_call(
        functools.partial(_fox_sample_body, npg=npg),
        grid_spec=grid_spec,
        out_shape=jax.ShapeDtypeStruct((nb, rows, HEAD_DIM), F32),
        compiler_params=_cparams(2),
        name="fox_sample",
    )(page_table.reshape(-1), qcat, *([ck] * npg), *([cv] * npg), *([clf] * npg), kn, vn, lfn)


def _gmlp_body(gu_ref, gv_ref, gg_ref, w_ref, bt_ref, yb_ref, *vn_ref, blk):
    r = gv_ref.shape[0]
    vn = _rms_rows(gv_ref[...], gg_ref[...])
    if vn_ref:
        vn_ref[0][...] = vn
    vnb = vn.astype(BF16)
    rr = lax.broadcasted_iota(jnp.int32, (r, r), 0)
    cc = lax.broadcasted_iota(jnp.int32, (r, r), 1)
    sh = blk.bit_length() - 1
    keep = jnp.logical_and((rr >> sh) == (cc >> sh), (cc & (blk - 1)) <= (rr & (blk - 1)))
    for g in range(GM_GROUPS):
        sl = slice(g * GM_GROUP_W, (g + 1) * GM_GROUP_W)
        w = jnp.where(keep, w_ref[g], jnp.zeros((), BF16))
        mixed = _dot(w, vnb[:, sl]) + bt_ref[:, g:g + 1]
        yb_ref[:, sl] = (gu_ref[:, sl].astype(F32) * mixed).astype(BF16)


def _gmlp(gu, gv, gg, w, bt, rows, blk, want_vn):
    m = gu.shape[0]
    row_spec = pl.BlockSpec((rows, GM_W), lambda i: (i, 0))
    out_shape = [jax.ShapeDtypeStruct((m, GM_W), BF16)]
    out_specs = [row_spec]
    if want_vn:
        out_shape.append(jax.ShapeDtypeStruct((m, GM_W), F32))
        out_specs.append(row_spec)
    return pl.pallas_call(
        functools.partial(_gmlp_body, blk=blk),
        grid=(m // rows,),
        in_specs=[row_spec, row_spec,
                  pl.BlockSpec((1, GM_W), lambda i: (0, 0)),
                  pl.BlockSpec((GM_GROUPS, rows, rows), lambda i: (0, 0, 0)),
                  pl.BlockSpec((rows, GM_GROUPS), lambda i: (0, 0))],
        out_specs=out_specs,
        out_shape=out_shape,
        compiler_params=_cparams(1),
        name="gmlp",
    )(gu, gv, gg, w, bt)


def _merge_body(ya_ref, yb_ref, woa_ref, wob_ref, sa_ref, sb_ref, m_ref):
    a = _dot(ya_ref[...], woa_ref[...])
    b = _dot(yb_ref[...], wob_ref[...])
    m_ref[...] = (sa_ref[...].astype(F32) * a + sb_ref[...].astype(F32) * b).astype(BF16)


def _merge(ya, yb, woa, wob, sa, sb, tm, tn=512):
    m, ka = ya.shape
    n = woa.shape[1]
    return pl.pallas_call(
        _merge_body,
        grid=(m // tm, n // tn),
        in_specs=[pl.BlockSpec((tm, ka), lambda i, j: (i, 0)),
                  pl.BlockSpec((tm, ka), lambda i, j: (i, 0)),
                  pl.BlockSpec((ka, tn), lambda i, j: (0, j)),
                  pl.BlockSpec((ka, tn), lambda i, j: (0, j)),
                  pl.BlockSpec((tm, tn), lambda i, j: (i, j)),
                  pl.BlockSpec((tm, tn), lambda i, j: (i, j))],
        out_specs=pl.BlockSpec((tm, tn), lambda i, j: (i, j)),
        out_shape=jax.ShapeDtypeStruct((m, n), BF16),
        compiler_params=_cparams(2),
        name="merge",
    )(ya, yb, woa, wob, sa, sb)


def _resid_mm_body(a_ref, w_ref, x_ref, o_ref):
    o_ref[...] = x_ref[...] + _dot(a_ref[...], w_ref[...])


def _resid_mm(a, w, x, tm, tn, name):
    m, k = a.shape
    n = w.shape[1]
    return pl.pallas_call(
        _resid_mm_body,
        grid=(m // tm, n // tn),
        in_specs=[pl.BlockSpec((tm, k), lambda i, j: (i, 0)),
                  pl.BlockSpec((k, tn), lambda i, j: (0, j)),
                  pl.BlockSpec((tm, tn), lambda i, j: (i, j))],
        out_specs=pl.BlockSpec((tm, tn), lambda i, j: (i, j)),
        out_shape=jax.ShapeDtypeStruct((m, n), F32),
        compiler_params=_cparams(2),
        name=name,
    )(a, w, x)


def _ffn_up_body(x_ref, g2_ref, wu_ref, wg_ref, cw_ref, cb_ref, *refs, seq_rows):
    i = pl.program_id(0)
    j = pl.program_id(1)
    if seq_rows is None:
        act_ref, st_ref, h_scr, carry_scr = refs
    else:
        prev_ref, act_ref, st_ref, h_scr = refs

    @pl.when(j == 0)
    def _():
        h_scr[...] = _rms_rows(x_ref[...], g2_ref[...]).astype(BF16)

    h = h_scr[...]
    a = _dot(h, wu_ref[...])
    gate = _dot(h, wg_ref[...])
    tm = a.shape[0]
    row = lax.broadcasted_iota(jnp.int32, a.shape, 0)
    if seq_rows is None:
        @pl.when(i == 0)
        def _():
            carry_scr[j] = jnp.zeros(carry_scr.shape[1:], F32)

        prev = carry_scr[j]
        p1 = prev[SUBLANES - 1:SUBLANES, :]
        p2 = prev[SUBLANES - 2:SUBLANES - 1, :]
        a1 = jnp.where(row == 0, p1, pltpu.roll(a, 1, axis=0))
        a2 = jnp.where(row == 0, p2, jnp.where(row == 1, p1, pltpu.roll(a, 2, axis=0)))
        carry_scr[j] = a[tm - SUBLANES:, :]
        st_ref[...] = a[tm - SUBLANES:, :]
    else:
        t = row & (seq_rows - 1)
        prev = prev_ref[...]
        a1 = jnp.where(t == 0, pltpu.roll(prev, tm - 1, axis=0), pltpu.roll(a, 1, axis=0))
        a2 = jnp.where(t < 2, prev, pltpu.roll(a, 2, axis=0))
        st_ref[...] = a
    c = cb_ref[...] + cw_ref[0:1, :] * a2 + cw_ref[1:2, :] * a1 + cw_ref[2:3, :] * a
    act_ref[...] = (c * _sigmoid(c) * gate).astype(BF16)


def _ffn_up(x, g2, wu, wg, cw, cb, tm, tn, prev=None, seq_rows=None):
    m, d = x.shape
    f = wu.shape[1]
    nj = f // tn
    col = lambda i, j: (0, j)
    in_specs = [pl.BlockSpec((tm, d), lambda i, j: (i, 0)),
                pl.BlockSpec((1, d), lambda i, j: (0, 0)),
                pl.BlockSpec((d, tn), col),
                pl.BlockSpec((d, tn), col),
                pl.BlockSpec((CONV_W, tn), col),
                pl.BlockSpec((1, tn), col)]
    args = [x, g2, wu, wg, cw, cb]
    scratch = [pltpu.VMEM((tm, d), BF16)]
    if seq_rows is None:
        st_shape = jax.ShapeDtypeStruct((m // tm * SUBLANES, f), F32)
        st_spec = pl.BlockSpec((SUBLANES, tn), lambda i, j: (i, j))
        scratch.append(pltpu.VMEM((nj, SUBLANES, tn), F32))
    else:
        in_specs.append(pl.BlockSpec((tm, tn), lambda i, j: (i, j)))
        args.append(prev)
        st_shape = jax.ShapeDtypeStruct((m, f), F32)
        st_spec = pl.BlockSpec((tm, tn), lambda i, j: (i, j))
    return pl.pallas_call(
        functools.partial(_ffn_up_body, seq_rows=seq_rows),
        grid=(m // tm, nj),
        in_specs=in_specs,
        out_specs=[pl.BlockSpec((tm, tn), lambda i, j: (i, j)), st_spec],
        out_shape=[jax.ShapeDtypeStruct((m, f), BF16), st_shape],
        scratch_shapes=scratch,
        compiler_params=_cparams(2),
        name="ffn_up",
    )(*args)


def _layer_dense_tail(x, ya, yb, sa, sb, wts, tm, ffn_tn, prev=None, seq_rows=None):
    mrg = _merge(ya, yb, wts["oa"], wts["ob"], sa, sb, tm)
    x1 = _resid_mm(mrg, wts["out"], x, tm, 512, "out_proj")
    act, st = _ffn_up(x1, wts["g2"], wts["up"], wts["gate"], wts["cw"], wts["cb"], tm, ffn_tn, prev, seq_rows)
    y = _resid_mm(act, wts["down"], x1, tm, 256, "ffn_down")
    return y, st


def kernel(x_prompt, x_sample, cache_k, cache_v, cache_logf, state_conv, page_table, norm1_g, w_in, b_f,
           q_norm_g, k_norm_g, gm_norm_g, w_s, b_s, w_oa, w_ob, w_out, norm2_g, w_up, w_gate, conv_w, conv_b,
           w_down):
    depth = w_in.shape[0]
    assert depth == 1
    bp, sp, d = x_prompt.shape
    assert bp == 1
    bs, ts, _ = x_sample.shape
    assert ts == SUBLANES
    f = w_up.shape[-1]
    l = 0

    qkv_w = 3 * ATT_W
    wi = w_in[l]
    wa = jnp.concatenate([wi[:, :qkv_w], wi[:, qkv_w + N_HEADS:]], axis=1).astype(BF16)
    wf = jnp.pad(wi[:, qkv_w:qkv_w + N_HEADS], ((0, 0), (0, LANES - N_HEADS))).astype(BF16)
    bfp = jnp.pad(b_f[l], (0, LANES - N_HEADS)).reshape(1, LANES)
    g1 = norm1_g[l].reshape(1, d)
    qg = q_norm_g[l].reshape(1, HEAD_DIM)
    kg = k_norm_g[l].reshape(1, HEAD_DIM)
    gg = gm_norm_g[l].reshape(1, GM_W)
    wts = {"oa": w_oa[l].astype(BF16), "ob": w_ob[l].astype(BF16), "out": w_out[l].astype(BF16),
           "g2": norm2_g[l].reshape(1, d), "up": w_up[l].astype(BF16), "gate": w_gate[l].astype(BF16),
           "cw": conv_w[l], "cb": conv_b[l].reshape(1, f), "down": w_down[l].astype(BF16)}
    ws = w_s[l].astype(BF16)

    xp = x_prompt.reshape(sp, d)
    q, kf, kb, vf, vb, gu, gv, sa, sb, lf = _inproj(xp, g1, wa, wf, bfp, qg, kg, tm=512)
    ya = _fox_prompt(q, kb, vb, _negcumsum(lf))
    (yb,) = _gmlp(gu, gv, gg, ws, jnp.transpose(b_s[l]), CHUNK, CHUNK, False)
    yp, st_p = _layer_dense_tail(xp, ya, yb, sa, sb, wts, 512, 256)

    ms = bs * ts
    xs = x_sample.reshape(ms, d)
    q, kf_s, kb_s, vf_s, vb_s, gu, gv, sa, sb, lf_s = _inproj(xs, g1, wa, wf, bfp, qg, kg, tm=ms)
    rows = N_HEADS * ts
    qcat = jnp.transpose(q.reshape(bs, ts, N_HEADS, HEAD_DIM), (0, 2, 1, 3)).reshape(bs, rows, HEAD_DIM)
    n_pool = cache_k.shape[1]
    page_rows = cache_k.shape[2] * N_HEADS
    ck = cache_k[l].reshape(n_pool, page_rows, HEAD_DIM)
    cv = cache_v[l].reshape(n_pool, page_rows, HEAD_DIM)
    clf = cache_logf[l].reshape(n_pool, page_rows // LANES, LANES)
    o = _fox_sample(page_table, qcat, ck, cv, clf,
                    kb_s.reshape(bs, rows, HEAD_DIM), vb_s.reshape(bs, rows, HEAD_DIM),
                    lf_s[:, :N_HEADS].reshape(bs, 1, rows))
    ya = jnp.transpose(o.reshape(bs, N_HEADS, ts, HEAD_DIM), (0, 2, 1, 3)).reshape(ms, ATT_W).astype(BF16)
    w_small = jnp.tile(ws[:, :ts, :ts], (1, bs, bs))
    bt_small = jnp.tile(jnp.transpose(b_s[l][:, :ts]), (bs, 1))
    yb, vn = _gmlp(gu, gv, gg, w_small, bt_small, ms, ts, True)
    prev = jnp.pad(state_conv[l], ((0, 0), (0, ts - (CONV_W - 1)), (0, 0))).reshape(ms, f)
    ys, a_s = _layer_dense_tail(xs, ya, yb, sa, sb, wts, ms, 256, prev, ts)

    return (yp.reshape(1, sp, d),
            ys.reshape(bs, ts, d),
            kf.reshape(1, 1, sp, N_HEADS, HEAD_DIM),
            vf.reshape(1, 1, sp, N_HEADS, HEAD_DIM),
            lf[:, :N_HEADS].reshape(1, 1, sp, N_HEADS),
            st_p[st_p.shape[0] - (CONV_W - 1):].reshape(1, 1, CONV_W - 1, f),
            kf_s.reshape(1, bs, ts, N_HEADS, HEAD_DIM),
            vf_s.reshape(1, bs, ts, N_HEADS, HEAD_DIM),
            lf_s[:, :N_HEADS].reshape(1, bs, ts, N_HEADS),
            vn.reshape(1, bs, ts, GM_W),
            a_s.reshape(bs, ts, f)[:, ts - (CONV_W - 1):].reshape(1, bs, CONV_W - 1, f))
```

```python
import functools

import jax
import jax.numpy as jnp
import numpy as np
from jax import lax
from jax.experimental import pallas as pl
from jax.experimental.pallas import tpu as pltpu

F32 = jnp.float32
BF16 = jnp.bfloat16

LANES = 128
SUBLANES = 8
VMEM_LIMIT = 56 * 1024 * 1024

HEAD_DIM = 128
N_HEADS = 16
ATT_W = N_HEADS * HEAD_DIM
GM_GROUPS = 16
GM_GROUP_W = 128
GM_W = GM_GROUPS * GM_GROUP_W
CHUNK = 128
CONV_W = 3
EPS = 1e-6
NEG = -1e30
SCALE = HEAD_DIM ** -0.5
LOG2E = float(np.log2(np.e))
Q_FOLD = SCALE * LOG2E

PROJ_TN = 512
PAGES_PER_STEP = 4


def _cparams(n_axes):
    return pltpu.CompilerParams(dimension_semantics=("arbitrary",) * n_axes, vmem_limit_bytes=VMEM_LIMIT)


def _dot(a, b):
    return jnp.dot(a, b, preferred_element_type=F32)


def _dot_nt(a, b):
    return lax.dot_general(a, b, (((1,), (1,)), ((), ())), preferred_element_type=F32)


def _rms_rows(x, g):
    return x * lax.rsqrt(jnp.mean(x * x, axis=-1, keepdims=True) + EPS) * g


def _sigmoid(x):
    return 1.0 / (1.0 + jnp.exp(-x))


def _gelu_tanh(x):
    return 0.5 * x * (1.0 + jnp.tanh(np.sqrt(2.0 / np.pi).astype(np.float32) * (x + 0.044715 * (x * x * x))))


def _log_sigmoid(x):
    return jnp.minimum(x, 0.0) - jnp.log(1.0 + jnp.exp(-jnp.abs(x)))


_SEC = {"q": (0, 4), "k": (4, 4), "v": (8, 4), "u": (12, 4), "vg": (16, 4), "ga": (20, 8), "gb": (28, 8)}


_N_QKV = 12
ROW_CHAINS = 4


def _inproj_body(x_ref, g1_ref, wq_ref, wr_ref, wf_ref, bf_ref, qg_ref, kg_ref,
                 q_ref, kf_ref, kb_ref, vf_ref, vb_ref, gu_ref, gv_ref, sa_ref, sb_ref, lf_ref, h_scr):
    j = pl.program_id(1)

    @pl.when(j == 0)
    def _():
        hb = _rms_rows(x_ref[...], g1_ref[...]).astype(BF16)
        h_scr[...] = hb
        lf_ref[...] = _log_sigmoid(_dot(hb, wf_ref[...]) + bf_ref[...])

    ch = h_scr.shape[0] // ROW_CHAINS

    def section(name, w_ref, epilogue):
        lo, n = _SEC[name]

        @pl.when(jnp.logical_and(j >= lo, j < lo + n))
        def _():
            w = w_ref[...]
            for c in range(ROW_CHAINS):
                rows = slice(c * ch, (c + 1) * ch)
                epilogue(rows, _dot(h_scr[rows, :], w))

    heads = [slice(h * HEAD_DIM, (h + 1) * HEAD_DIM) for h in range(PROJ_TN // HEAD_DIM)]

    def q_epi(rows, acc):
        for sl in heads:
            q_ref[rows, sl] = (_rms_rows(acc[:, sl], qg_ref[...]) * Q_FOLD).astype(BF16)

    def k_epi(rows, acc):
        for sl in heads:
            y = _rms_rows(acc[:, sl], kg_ref[...])
            kf_ref[rows, sl] = y
            kb_ref[rows, sl] = y.astype(BF16)

    def v_epi(rows, acc):
        vf_ref[rows, :] = acc
        vb_ref[rows, :] = acc.astype(BF16)

    def u_epi(rows, acc):
        gu_ref[rows, :] = _gelu_tanh(acc).astype(BF16)

    def vg_epi(rows, acc):
        gv_ref[rows, :] = _gelu_tanh(acc)

    def ga_epi(rows, acc):
        sa_ref[rows, :] = _sigmoid(acc).astype(BF16)

    def gb_epi(rows, acc):
        sb_ref[rows, :] = _sigmoid(acc).astype(BF16)

    section("q", wq_ref, q_epi)
    section("k", wq_ref, k_epi)
    section("v", wq_ref, v_epi)
    section("u", wr_ref, u_epi)
    section("vg", wr_ref, vg_epi)
    section("ga", wr_ref, ga_epi)
    section("gb", wr_ref, gb_epi)


def _inproj(x, g1, wq, wr, wf, bfp, qg, kg, tm):
    m, d = x.shape
    nq = wq.shape[1] // PROJ_TN
    nr = wr.shape[1] // PROJ_TN
    assert nq == _N_QKV

    def sec(name):
        lo, n = _SEC[name]
        return pl.BlockSpec((tm, PROJ_TN), lambda i, j: (i, jnp.clip(j - lo, 0, n - 1)))

    def shp(name, dt):
        return jax.ShapeDtypeStruct((m, _SEC[name][1] * PROJ_TN), dt)

    const = lambda i, j: (0, 0)
    return pl.pallas_call(
        _inproj_body,
        grid=(m // tm, nq + nr),
        in_specs=[pl.BlockSpec((tm, d), lambda i, j: (i, 0)),
                  pl.BlockSpec((1, d), const),
                  pl.BlockSpec((d, PROJ_TN), lambda i, j: (0, jnp.minimum(j, nq - 1))),
                  pl.BlockSpec((d, PROJ_TN), lambda i, j: (0, jnp.maximum(j - nq, 0))),
                  pl.BlockSpec((d, LANES), const),
                  pl.BlockSpec((1, LANES), const),
                  pl.BlockSpec((1, HEAD_DIM), const),
                  pl.BlockSpec((1, HEAD_DIM), const)],
        out_specs=[sec("q"), sec("k"), sec("k"), sec("v"), sec("v"), sec("u"), sec("vg"), sec("ga"), sec("gb"),
                   pl.BlockSpec((tm, LANES), lambda i, j: (i, 0))],
        out_shape=[shp("q", BF16), shp("k", F32), shp("k", BF16), shp("v", F32), shp("v", BF16),
                   shp("u", BF16), shp("vg", F32), shp("ga", BF16), shp("gb", BF16),
                   jax.ShapeDtypeStruct((m, LANES), F32)],
        scratch_shapes=[pltpu.VMEM((tm, d), BF16)],
        compiler_params=_cparams(2),
        name="inproj",
    )(x, g1, wq, wr, wf, bfp, qg, kg)


def _lane_cumsum(x, shifts, lane):
    for sh in shifts:
        x = x + jnp.where(lane >= sh, pltpu.roll(x, sh, axis=1), 0.0)
    return x


def _negcumsum_body(lf_ref, o_ref):
    nblk = lf_ref.shape[0] // LANES
    lane = lax.broadcasted_iota(jnp.int32, (N_HEADS, LANES), 1)

    def body(b, carry):
        off = pl.multiple_of(b * LANES, LANES)
        xt = jnp.transpose(lf_ref[pl.ds(off, LANES), :])[:N_HEADS, :]
        c = _lane_cumsum(xt, (1, 2, 4, 8, 16, 32, 64), lane) + carry
        o_ref[:, pl.ds(off, LANES)] = c * (-LOG2E)
        return c[:, LANES - 1:LANES]

    lax.fori_loop(0, nblk, body, jnp.zeros((N_HEADS, 1), F32))


def _negcumsum(lf):
    s = lf.shape[0]
    return pl.pallas_call(
        _negcumsum_body,
        out_shape=jax.ShapeDtypeStruct((N_HEADS, s), F32),
        compiler_params=pltpu.CompilerParams(vmem_limit_bytes=VMEM_LIMIT),
        name="negcumsum",
    )(lf)


def _fox_prompt_body(q_ref, k_ref, v_ref, nc_ref, o_ref, m_scr, l_scr, acc_scr, *, tq, rq):
    qi = pl.program_id(1)
    nrc = tq // rq
    m_scr[...] = jnp.full(m_scr.shape, NEG, F32)
    l_scr[...] = jnp.zeros(l_scr.shape, F32)
    acc_scr[...] = jnp.zeros(acc_scr.shape, F32)

    def update(rc, z, v):
        rows = slice(rc * rq, (rc + 1) * rq)
        m_old = m_scr[rows, :]
        m_new = jnp.maximum(m_old, jnp.max(z, axis=-1, keepdims=True))
        alpha = jnp.exp2(m_old - m_new)
        pr = jnp.exp2(z - m_new)
        l_scr[rows, :] = alpha * l_scr[rows, :] + jnp.sum(pr, axis=-1, keepdims=True)
        acc_scr[rows, :] = alpha * acc_scr[rows, :] + _dot(pr.astype(BF16), v)
        m_scr[rows, :] = m_new

    def below_diagonal(ki, carry):
        off = pl.multiple_of(ki * tq, tq)
        k = k_ref[pl.ds(off, tq), :]
        v = v_ref[pl.ds(off, tq), :]
        nc = nc_ref[:, pl.ds(off, tq)]
        for rc in range(nrc):
            update(rc, _dot_nt(q_ref[rc * rq:(rc + 1) * rq, :], k) + nc, v)
        return carry

    lax.fori_loop(0, qi, below_diagonal, 0)

    off = pl.multiple_of(qi * tq, tq)
    for rc in range(nrc):
        n = (rc + 1) * rq
        z = _dot_nt(q_ref[rc * rq:(rc + 1) * rq, :], k_ref[pl.ds(off, n), :]) + nc_ref[:, pl.ds(off, n)]
        row = rc * rq + lax.broadcasted_iota(jnp.int32, (rq, n), 0)
        col = lax.broadcasted_iota(jnp.int32, (rq, n), 1)
        update(rc, jnp.where(row >= col, z, NEG), v_ref[pl.ds(off, n), :])
    o_ref[...] = (acc_scr[...] / l_scr[...]).astype(o_ref.dtype)


def _fox_prompt(qb, kb, vb, negc, tq=1024, rq=256):
    s = qb.shape[0]
    head = lambda h, i: (0, h)
    return pl.pallas_call(
        functools.partial(_fox_prompt_body, tq=tq, rq=rq),
        grid=(N_HEADS, s // tq),
        in_specs=[pl.BlockSpec((tq, HEAD_DIM), lambda h, i: (i, h)),
                  pl.BlockSpec((s, HEAD_DIM), head),
                  pl.BlockSpec((s, HEAD_DIM), head),
                  pl.BlockSpec((None, 1, s), lambda h, i: (h, 0, 0))],
        out_specs=pl.BlockSpec((tq, HEAD_DIM), lambda h, i: (i, h)),
        out_shape=jax.ShapeDtypeStruct((s, ATT_W), BF16),
        scratch_shapes=[pltpu.VMEM((tq, 1), F32), pltpu.VMEM((tq, 1), F32), pltpu.VMEM((tq, HEAD_DIM), F32)],
        compiler_params=_cparams(2),
        name="fox_prompt",
    )(qb, kb, vb, negc.reshape(N_HEADS, 1, s))


def _page_negcumsum(lf, run, lane, rowj):
    x = _lane_cumsum(lf, (16, 32, 64), lane)
    t = jnp.where(lane >= LANES - N_HEADS, x, 0.0)
    for sh in (16, 32, 64):
        t = t + pltpu.roll(t, sh, axis=1)
    e = t
    for sh in (1, 2, 4, 8):
        e = e + jnp.where(rowj >= sh, pltpu.roll(e, sh, axis=0), 0.0)
    c = x + (e - t) + run
    return -c, run + e[N_HEADS - 1:N_HEADS, :]


def _fox_sample_body(pt_ref, q_ref, *refs, npg):
    del pt_ref
    k_refs, v_refs, lf_refs = refs[:npg], refs[npg:2 * npg], refs[2 * npg:3 * npg]
    kn_ref, vn_ref, lfn_ref, o_ref, m_scr, l_scr, acc_scr, run_scr = refs[3 * npg:]
    g = pl.program_id(1)
    rows = N_HEADS * SUBLANES
    tiles = k_refs[0].shape[0] // LANES

    @pl.when(g == 0)
    def _():
        m_scr[...] = jnp.full(m_scr.shape, NEG, F32)
        l_scr[...] = jnp.zeros(l_scr.shape, F32)
        acc_scr[...] = jnp.zeros(acc_scr.shape, F32)
        run_scr[...] = jnp.zeros(run_scr.shape, F32)

    lane = lax.broadcasted_iota(jnp.int32, (rows, LANES), 1)
    row = lax.broadcasted_iota(jnp.int32, (rows, LANES), 0)
    same_head = (lane & (N_HEADS - 1)) == (row >> 3)
    lane16 = lax.broadcasted_iota(jnp.int32, (N_HEADS, LANES), 1)
    row16 = lax.broadcasted_iota(jnp.int32, (N_HEADS, LANES), 0)
    q = q_ref[...]

    def attend(zs, vbs, m, l, acc):
        zmax = zs[0]
        for z in zs[1:]:
            zmax = jnp.maximum(zmax, z)
        m_new = jnp.maximum(m, jnp.max(zmax, axis=-1, keepdims=True))
        alpha = jnp.exp2(m - m_new)
        per = len(zs) // len(vbs)
        psum = None
        pv = None
        for i, vb in enumerate(vbs):
            ps = [jnp.exp2(z - m_new) for z in zs[i * per:(i + 1) * per]]
            for pp in ps:
                psum = pp if psum is None else psum + pp
            pm = ps[0] if per == 1 else jnp.concatenate(ps, axis=1)
            d = _dot(pm.astype(BF16), vb)
            pv = d if pv is None else pv + d
        l = alpha * l + jnp.sum(psum, axis=-1, keepdims=True)
        return m_new, l, alpha * acc + pv

    run = run_scr[...]
    zs = []
    for p in range(npg):
        gt = _dot_nt(q, k_refs[p][...].astype(BF16))
        negc, run = _page_negcumsum(lf_refs[p][...], run, lane16, row16)
        negc = negc * LOG2E
        zs += [jnp.where(same_head, gt[:, t * LANES:(t + 1) * LANES] + negc[t:t + 1, :], NEG)
               for t in range(tiles)]
    m, l, acc = attend(zs, [v_refs[p][...].astype(BF16) for p in range(npg)],
                       m_scr[...], l_scr[...], acc_scr[...])
    m_scr[...], l_scr[...], acc_scr[...], run_scr[...] = m, l, acc, run

    @pl.when(g == pl.num_programs(1) - 1)
    def _():
        cn = _lane_cumsum(lfn_ref[...], (16, 32, 64), lane16[:1]) + run
        zn = _dot_nt(q, kn_ref[...]) - cn * LOG2E
        keep = jnp.logical_and(same_head, (lane >> 4) <= (row & (SUBLANES - 1)))
        _, l2, acc2 = attend([jnp.where(keep, zn, NEG)], [vn_ref[...]], m, l, acc)
        o_ref[...] = acc2 / l2


def _fox_sample(page_table, qcat, ck, cv, clf, kn, vn, lfn):
    nb = qcat.shape[0]
    n_pages = page_table.shape[1]
    npg = PAGES_PER_STEP
    rows = N_HEADS * SUBLANES
    prow = ck.shape[1]

    def page(p):
        return lambda b, g, pt: (pt[b * n_pages + g * npg + p], 0, 0)

    per_b = lambda b, g, pt: (b, 0, 0)
    in_specs = ([pl.BlockSpec((None, rows, HEAD_DIM), per_b)]
                + [pl.BlockSpec((None, prow, HEAD_DIM), page(p)) for p in range(npg)]
                + [pl.BlockSpec((None, prow, HEAD_DIM), page(p)) for p in range(npg)]
                + [pl.BlockSpec((None, N_HEADS, LANES), page(p)) for p in range(npg)]
                + [pl.BlockSpec((None, rows, HEAD_DIM), per_b),
                   pl.BlockSpec((None, rows, HEAD_DIM), per_b),
                   pl.BlockSpec((None, 1, LANES), per_b)])
    grid_spec = pltpu.PrefetchScalarGridSpec(
        num_scalar_prefetch=1,
        grid=(nb, n_pages // npg),
        in_specs=in_specs,
        out_specs=pl.BlockSpec((None, rows, HEAD_DIM), per_b),
        scratch_shapes=[pltpu.VMEM((rows, 1), F32), pltpu.VMEM((rows, 1), F32),
                        pltpu.VMEM((rows, HEAD_DIM), F32), pltpu.VMEM((1, LANES), F32)],
    )
    return pl.pallas_call(
        functools.partial(_fox_sample_body, npg=npg),
        grid_spec=grid_spec,
        out_shape=jax.ShapeDtypeStruct((nb, rows, HEAD_DIM), F32),
        compiler_params=_cparams(2),
        name="fox_sample",
    )(page_table.reshape(-1), qcat, *([ck] * npg), *([cv] * npg), *([clf] * npg), kn, vn, lfn)


def _gmlp_body(gu_ref, gv_ref, gg_ref, w_ref, bt_ref, yb_ref, *vn_ref, blk):
    r = gv_ref.shape[0]
    vn = _rms_rows(gv_ref[...], gg_ref[...])
    if vn_ref:
        vn_ref[0][...] = vn
    vnb = vn.astype(BF16)
    rr = lax.broadcasted_iota(jnp.int32, (r, r), 0)
    cc = lax.broadcasted_iota(jnp.int32, (r, r), 1)
    sh = blk.bit_length() - 1
    keep = jnp.logical_and((rr >> sh) == (cc >> sh), (cc & (blk - 1)) <= (rr & (blk - 1)))
    for g in range(GM_GROUPS):
        sl = slice(g * GM_GROUP_W, (g + 1) * GM_GROUP_W)
        w = jnp.where(keep, w_ref[g], jnp.zeros((), BF16))
        mixed = _dot(w, vnb[:, sl]) + bt_ref[:, g:g + 1]
        yb_ref[:, sl] = (gu_ref[:, sl].astype(F32) * mixed).astype(BF16)


def _gmlp(gu, gv, gg, w, bt, rows, blk, want_vn):
    m = gu.shape[0]
    row_spec = pl.BlockSpec((rows, GM_W), lambda i: (i, 0))
    out_shape = [jax.ShapeDtypeStruct((m, GM_W), BF16)]
    out_specs = [row_spec]
    if want_vn:
        out_shape.append(jax.ShapeDtypeStruct((m, GM_W), F32))
        out_specs.append(row_spec)
    return pl.pallas_call(
        functools.partial(_gmlp_body, blk=blk),
        grid=(m // rows,),
        in_specs=[row_spec, row_spec,
                  pl.BlockSpec((1, GM_W), lambda i: (0, 0)),
                  pl.BlockSpec((GM_GROUPS, rows, rows), lambda i: (0, 0, 0)),
                  pl.BlockSpec((rows, GM_GROUPS), lambda i: (0, 0))],
        out_specs=out_specs,
        out_shape=out_shape,
        compiler_params=_cparams(1),
        name="gmlp",
    )(gu, gv, gg, w, bt)


def _merge_body(ya_ref, yb_ref, woa_ref, wob_ref, sa_ref, sb_ref, m_ref):
    a = _dot(ya_ref[...], woa_ref[...])
    b = _dot(yb_ref[...], wob_ref[...])
    m_ref[...] = (sa_ref[...].astype(F32) * a + sb_ref[...].astype(F32) * b).astype(BF16)


def _merge(ya, yb, woa, wob, sa, sb, tm, tn=512):
    m, ka = ya.shape
    n = woa.shape[1]
    return pl.pallas_call(
        _merge_body,
        grid=(m // tm, n // tn),
        in_specs=[pl.BlockSpec((tm, ka), lambda i, j: (i, 0)),
                  pl.BlockSpec((tm, ka), lambda i, j: (i, 0)),
                  pl.BlockSpec((ka, tn), lambda i, j: (0, j)),
                  pl.BlockSpec((ka, tn), lambda i, j: (0, j)),
                  pl.BlockSpec((tm, tn), lambda i, j: (i, j)),
                  pl.BlockSpec((tm, tn), lambda i, j: (i, j))],
        out_specs=pl.BlockSpec((tm, tn), lambda i, j: (i, j)),
        out_shape=jax.ShapeDtypeStruct((m, n), BF16),
        compiler_params=_cparams(2),
        name="merge",
    )(ya, yb, woa, wob, sa, sb)


def _resid_mm_body(a_ref, w_ref, x_ref, o_ref):
    o_ref[...] = x_ref[...] + _dot(a_ref[...], w_ref[...])


def _resid_mm(a, w, x, tm, tn, name):
    m, k = a.shape
    n = w.shape[1]
    return pl.pallas_call(
        _resid_mm_body,
        grid=(m // tm, n // tn),
        in_specs=[pl.BlockSpec((tm, k), lambda i, j: (i, 0)),
                  pl.BlockSpec((k, tn), lambda i, j: (0, j)),
                  pl.BlockSpec((tm, tn), lambda i, j: (i, j))],
        out_specs=pl.BlockSpec((tm, tn), lambda i, j: (i, j)),
        out_shape=jax.ShapeDtypeStruct((m, n), F32),
        compiler_params=_cparams(2),
        name=name,
    )(a, w, x)


def _ffn_up_body(x_ref, g2_ref, wu_ref, wg_ref, cw_ref, cb_ref, *refs, seq_rows):
    i = pl.program_id(0)
    j = pl.program_id(1)
    if seq_rows is None:
        act_ref, st_ref, h_scr, carry_scr = refs
    else:
        prev_ref, act_ref, st_ref, h_scr = refs

    @pl.when(j == 0)
    def _():
        h_scr[...] = _rms_rows(x_ref[...], g2_ref[...]).astype(BF16)

    ch = h_scr.shape[0] // ROW_CHAINS
    wu = wu_ref[...]
    wg = wg_ref[...]
    row = lax.broadcasted_iota(jnp.int32, (ch, wu.shape[1]), 0)
    if seq_rows is None:
        @pl.when(i == 0)
        def _():
            carry_scr[j] = jnp.zeros(carry_scr.shape[1:], F32)

        above = carry_scr[j]
    for ci in range(ROW_CHAINS):
        rows = slice(ci * ch, (ci + 1) * ch)
        h = h_scr[rows, :]
        a = _dot(h, wu)
        gate = _dot(h, wg)
        if seq_rows is None:
            p1 = above[SUBLANES - 1:SUBLANES, :]
            p2 = above[SUBLANES - 2:SUBLANES - 1, :]
            a1 = jnp.where(row == 0, p1, pltpu.roll(a, 1, axis=0))
            a2 = jnp.where(row == 0, p2, jnp.where(row == 1, p1, pltpu.roll(a, 2, axis=0)))
            above = a[ch - SUBLANES:, :]
        else:
            t = row & (seq_rows - 1)
            prev = prev_ref[rows, :]
            a1 = jnp.where(t == 0, pltpu.roll(prev, ch - 1, axis=0), pltpu.roll(a, 1, axis=0))
            a2 = jnp.where(t < 2, prev, pltpu.roll(a, 2, axis=0))
            st_ref[rows, :] = a
        c = cb_ref[...] + cw_ref[0:1, :] * a2 + cw_ref[1:2, :] * a1 + cw_ref[2:3, :] * a
        act_ref[rows, :] = (c * _sigmoid(c) * gate).astype(BF16)
    if seq_rows is None:
        carry_scr[j] = above
        st_ref[...] = above


def _ffn_up(x, g2, wu, wg, cw, cb, tm, tn, prev=None, seq_rows=None):
    m, d = x.shape
    f = wu.shape[1]
    nj = f // tn
    col = lambda i, j: (0, j)
    in_specs = [pl.BlockSpec((tm, d), lambda i, j: (i, 0)),
                pl.BlockSpec((1, d), lambda i, j: (0, 0)),
                pl.BlockSpec((d, tn), col),
                pl.BlockSpec((d, tn), col),
                pl.BlockSpec((CONV_W, tn), col),
                pl.BlockSpec((1, tn), col)]
    args = [x, g2, wu, wg, cw, cb]
    scratch = [pltpu.VMEM((tm, d), BF16)]
    if seq_rows is None:
        st_shape = jax.ShapeDtypeStruct((m // tm * SUBLANES, f), F32)
        st_spec = pl.BlockSpec((SUBLANES, tn), lambda i, j: (i, j))
        scratch.append(pltpu.VMEM((nj, SUBLANES, tn), F32))
    else:
        in_specs.append(pl.BlockSpec((tm, tn), lambda i, j: (i, j)))
        args.append(prev)
        st_shape = jax.ShapeDtypeStruct((m, f), F32)
        st_spec = pl.BlockSpec((tm, tn), lambda i, j: (i, j))
    return pl.pallas_call(
        functools.partial(_ffn_up_body, seq_rows=seq_rows),
        grid=(m // tm, nj),
        in_specs=in_specs,
        out_specs=[pl.BlockSpec((tm, tn), lambda i, j: (i, j)), st_spec],
        out_shape=[jax.ShapeDtypeStruct((m, f), BF16), st_shape],
        scratch_shapes=scratch,
        compiler_params=_cparams(2),
        name="ffn_up",
    )(*args)


def _layer_dense_tail(x, ya, yb, sa, sb, wts, tm, ffn_tn, prev=None, seq_rows=None):
    mrg = _merge(ya, yb, wts["oa"], wts["ob"], sa, sb, tm)
    x1 = _resid_mm(mrg, wts["out"], x, tm, 512, "out_proj")
    act, st = _ffn_up(x1, wts["g2"], wts["up"], wts["gate"], wts["cw"], wts["cb"], tm, ffn_tn, prev, seq_rows)
    y = _resid_mm(act, wts["down"], x1, tm, 256, "ffn_down")
    return y, st


def kernel(x_prompt, x_sample, cache_k, cache_v, cache_logf, state_conv, page_table, norm1_g, w_in, b_f,
           q_norm_g, k_norm_g, gm_norm_g, w_s, b_s, w_oa, w_ob, w_out, norm2_g, w_up, w_gate, conv_w, conv_b,
           w_down):
    depth = w_in.shape[0]
    assert depth == 1
    bp, sp, d = x_prompt.shape
    assert bp == 1
    bs, ts, _ = x_sample.shape
    assert ts == SUBLANES
    f = w_up.shape[-1]
    l = 0

    qkv_w = 3 * ATT_W
    wi = w_in[l]
    wq = wi[:, :qkv_w].astype(BF16)
    wr = wi[:, qkv_w + N_HEADS:].astype(BF16)
    wf = jnp.pad(wi[:, qkv_w:qkv_w + N_HEADS], ((0, 0), (0, LANES - N_HEADS))).astype(BF16)
    bfp = jnp.pad(b_f[l], (0, LANES - N_HEADS)).reshape(1, LANES)
    g1 = norm1_g[l].reshape(1, d)
    qg = q_norm_g[l].reshape(1, HEAD_DIM)
    kg = k_norm_g[l].reshape(1, HEAD_DIM)
    gg = gm_norm_g[l].reshape(1, GM_W)
    wts = {"oa": w_oa[l].astype(BF16), "ob": w_ob[l].astype(BF16), "out": w_out[l].astype(BF16),
           "g2": norm2_g[l].reshape(1, d), "up": w_up[l].astype(BF16), "gate": w_gate[l].astype(BF16),
           "cw": conv_w[l], "cb": conv_b[l].reshape(1, f), "down": w_down[l].astype(BF16)}
    ws = w_s[l].astype(BF16)

    xp = x_prompt.reshape(sp, d)
    q, kf, kb, vf, vb, gu, gv, sa, sb, lf = _inproj(xp, g1, wq, wr, wf, bfp, qg, kg, tm=512)
    ya = _fox_prompt(q, kb, vb, _negcumsum(lf))
    (yb,) = _gmlp(gu, gv, gg, ws, jnp.transpose(b_s[l]), CHUNK, CHUNK, False)
    yp, st_p = _layer_dense_tail(xp, ya, yb, sa, sb, wts, 512, 256)

    ms = bs * ts
    xs = x_sample.reshape(ms, d)
    q, kf_s, kb_s, vf_s, vb_s, gu, gv, sa, sb, lf_s = _inproj(xs, g1, wq, wr, wf, bfp, qg, kg, tm=ms)
    rows = N_HEADS * ts
    qcat = jnp.transpose(q.reshape(bs, ts, N_HEADS, HEAD_DIM), (0, 2, 1, 3)).reshape(bs, rows, HEAD_DIM)
    n_pool = cache_k.shape[1]
    page_rows = cache_k.shape[2] * N_HEADS
    ck = cache_k[l].reshape(n_pool, page_rows, HEAD_DIM)
    cv = cache_v[l].reshape(n_pool, page_rows, HEAD_DIM)
    clf = cache_logf[l].reshape(n_pool, page_rows // LANES, LANES)
    o = _fox_sample(page_table, qcat, ck, cv, clf,
                    kb_s.reshape(bs, rows, HEAD_DIM), vb_s.reshape(bs, rows, HEAD_DIM),
                    lf_s[:, :N_HEADS].reshape(bs, 1, rows))
    ya = jnp.transpose(o.reshape(bs, N_HEADS, ts, HEAD_DIM), (0, 2, 1, 3)).reshape(ms, ATT_W).astype(BF16)
    w_small = jnp.tile(ws[:, :ts, :ts], (1, bs, bs))
    bt_small = jnp.tile(jnp.transpose(b_s[l][:, :ts]), (bs, 1))
    yb, vn = _gmlp(gu, gv, gg, w_small, bt_small, ms, ts, True)
    prev = jnp.pad(state_conv[l], ((0, 0), (0, ts - (CONV_W - 1)), (0, 0))).reshape(ms, f)
    ys, a_s = _layer_dense_tail(xs, ya, yb, sa, sb, wts, ms, 256, prev, ts)

    return (yp.reshape(1, sp, d),
            ys.reshape(bs, ts, d),
            kf.reshape(1, 1, sp, N_HEADS, HEAD_DIM),
            vf.reshape(1, 1, sp, N_HEADS, HEAD_DIM),
            lf[:, :N_HEADS].reshape(1, 1, sp, N_HEADS),
            st_p[st_p.shape[0] - (CONV_W - 1):].reshape(1, 1, CONV_W - 1, f),
            kf_s.reshape(1, bs, ts, N_HEADS, HEAD_DIM),
            vf_s.reshape(1, bs, ts, N_HEADS, HEAD_DIM),
            lf_s[:, :N_HEADS].reshape(1, bs, ts, N_HEADS),
            vn.reshape(1, bs, ts, GM_W),
            a_s.reshape(bs, ts, f)[:, ts - (CONV_W - 1):].reshape(1, bs, CONV_W - 1, f))
```

```python
import functools

import jax
import jax.numpy as jnp
import numpy as np
from jax import lax
from jax.experimental import pallas as pl
from jax.experimental.pallas import tpu as pltpu

F32 = jnp.float32
BF16 = jnp.bfloat16

LANES = 128
SUBLANES = 8
VMEM_LIMIT = 56 * 1024 * 1024

HEAD_DIM = 128
N_HEADS = 16
ATT_W = N_HEADS * HEAD_DIM
GM_GROUPS = 16
GM_GROUP_W = 128
GM_W = GM_GROUPS * GM_GROUP_W
CHUNK = 128
CONV_W = 3
EPS = 1e-6
NEG = -1e30
SCALE = HEAD_DIM ** -0.5
LOG2E = float(np.log2(np.e))
Q_FOLD = SCALE * LOG2E

PROJ_TN = 512
MERGE_TN = 512
FFN_TN = 256
PAGES_PER_STEP = 4


def _cparams(n_axes):
    return pltpu.CompilerParams(dimension_semantics=("arbitrary",) * n_axes, vmem_limit_bytes=VMEM_LIMIT)


def _dot(a, b):
    return jnp.dot(a, b, preferred_element_type=F32)


def _dot_nt(a, b):
    return lax.dot_general(a, b, (((1,), (1,)), ((), ())), preferred_element_type=F32)


def _rms_rows(x, g):
    return x * lax.rsqrt(jnp.mean(x * x, axis=-1, keepdims=True) + EPS) * g


def _sigmoid(x):
    return 1.0 / (1.0 + jnp.exp(-x))


def _gelu_tanh(x):
    return 0.5 * x * (1.0 + jnp.tanh(np.sqrt(2.0 / np.pi).astype(np.float32) * (x + 0.044715 * (x * x * x))))


def _log_sigmoid(x):
    return jnp.minimum(x, 0.0) - jnp.log(1.0 + jnp.exp(-jnp.abs(x)))


_SEC = {"q": (0, 4), "k": (4, 4), "v": (8, 4), "u": (12, 4), "vg": (16, 4), "ga": (20, 8), "gb": (28, 8)}


_N_QKV = 12
ROW_CHAINS = 1


def _inproj_body(x_ref, g1_ref, wq_ref, wr_ref, wf_ref, bf_ref, qg_ref, kg_ref,
                 q_ref, kf_ref, kb_ref, vf_ref, vb_ref, gu_ref, gv_ref, sa_ref, sb_ref, lf_ref, h_scr):
    j = pl.program_id(1)

    @pl.when(j == 0)
    def _():
        hb = _rms_rows(x_ref[...], g1_ref[...]).astype(BF16)
        h_scr[...] = hb
        lf_ref[...] = _log_sigmoid(_dot(hb, wf_ref[...]) + bf_ref[...])

    ch = h_scr.shape[0] // ROW_CHAINS

    def section(name, w_ref, epilogue):
        lo, n = _SEC[name]

        @pl.when(jnp.logical_and(j >= lo, j < lo + n))
        def _():
            w = w_ref[...]
            for c in range(ROW_CHAINS):
                rows = slice(c * ch, (c + 1) * ch)
                epilogue(rows, _dot(h_scr[rows, :], w))

    heads = [slice(h * HEAD_DIM, (h + 1) * HEAD_DIM) for h in range(PROJ_TN // HEAD_DIM)]

    def q_epi(rows, acc):
        for sl in heads:
            q_ref[rows, sl] = (_rms_rows(acc[:, sl], qg_ref[...]) * Q_FOLD).astype(BF16)

    def k_epi(rows, acc):
        for sl in heads:
            y = _rms_rows(acc[:, sl], kg_ref[...])
            kf_ref[rows, sl] = y
            kb_ref[rows, sl] = y.astype(BF16)

    def v_epi(rows, acc):
        vf_ref[rows, :] = acc
        vb_ref[rows, :] = acc.astype(BF16)

    def u_epi(rows, acc):
        gu_ref[rows, :] = _gelu_tanh(acc).astype(BF16)

    def vg_epi(rows, acc):
        gv_ref[rows, :] = _gelu_tanh(acc)

    def ga_epi(rows, acc):
        sa_ref[rows, :] = _sigmoid(acc).astype(BF16)

    def gb_epi(rows, acc):
        sb_ref[rows, :] = _sigmoid(acc).astype(BF16)

    section("q", wq_ref, q_epi)
    section("k", wq_ref, k_epi)
    section("v", wq_ref, v_epi)
    section("u", wr_ref, u_epi)
    section("vg", wr_ref, vg_epi)
    section("ga", wr_ref, ga_epi)
    section("gb", wr_ref, gb_epi)


def _tile_major(w, tn):
    k, n = w.shape
    return jnp.transpose(w.reshape(k, n // tn, tn), (1, 0, 2))


def _inproj(x, g1, wq, wr, wf, bfp, qg, kg, tm):
    m, d = x.shape
    nq = wq.shape[0]
    nr = wr.shape[0]
    assert nq == _N_QKV and wq.shape[2] == PROJ_TN and wr.shape[2] == PROJ_TN

    def sec(name):
        lo, n = _SEC[name]
        return pl.BlockSpec((tm, PROJ_TN), lambda i, j: (i, jnp.clip(j - lo, 0, n - 1)))

    def shp(name, dt):
        return jax.ShapeDtypeStruct((m, _SEC[name][1] * PROJ_TN), dt)

    const = lambda i, j: (0, 0)
    return pl.pallas_call(
        _inproj_body,
        grid=(m // tm, nq + nr),
        in_specs=[pl.BlockSpec((tm, d), lambda i, j: (i, 0)),
                  pl.BlockSpec((1, d), const),
                  pl.BlockSpec((None, d, PROJ_TN), lambda i, j: (jnp.minimum(j, nq - 1), 0, 0)),
                  pl.BlockSpec((None, d, PROJ_TN), lambda i, j: (jnp.maximum(j - nq, 0), 0, 0)),
                  pl.BlockSpec((d, LANES), const),
                  pl.BlockSpec((1, LANES), const),
                  pl.BlockSpec((1, HEAD_DIM), const),
                  pl.BlockSpec((1, HEAD_DIM), const)],
        out_specs=[sec("q"), sec("k"), sec("k"), sec("v"), sec("v"), sec("u"), sec("vg"), sec("ga"), sec("gb"),
                   pl.BlockSpec((tm, LANES), lambda i, j: (i, 0))],
        out_shape=[shp("q", BF16), shp("k", F32), shp("k", BF16), shp("v", F32), shp("v", BF16),
                   shp("u", BF16), shp("vg", F32), shp("ga", BF16), shp("gb", BF16),
                   jax.ShapeDtypeStruct((m, LANES), F32)],
        scratch_shapes=[pltpu.VMEM((tm, d), BF16)],
        compiler_params=_cparams(2),
        name="inproj",
    )(x, g1, wq, wr, wf, bfp, qg, kg)


def _lane_cumsum(x, shifts, lane):
    for sh in shifts:
        x = x + jnp.where(lane >= sh, pltpu.roll(x, sh, axis=1), 0.0)
    return x


def _negcumsum_body(lf_ref, o_ref):
    nblk = lf_ref.shape[0] // LANES
    lane = lax.broadcasted_iota(jnp.int32, (N_HEADS, LANES), 1)

    def body(b, carry):
        off = pl.multiple_of(b * LANES, LANES)
        xt = jnp.transpose(lf_ref[pl.ds(off, LANES), :])[:N_HEADS, :]
        c = _lane_cumsum(xt, (1, 2, 4, 8, 16, 32, 64), lane) + carry
        o_ref[:, pl.ds(off, LANES)] = c * (-LOG2E)
        return c[:, LANES - 1:LANES]

    lax.fori_loop(0, nblk, body, jnp.zeros((N_HEADS, 1), F32))


def _negcumsum(lf):
    s = lf.shape[0]
    return pl.pallas_call(
        _negcumsum_body,
        out_shape=jax.ShapeDtypeStruct((N_HEADS, s), F32),
        compiler_params=pltpu.CompilerParams(vmem_limit_bytes=VMEM_LIMIT),
        name="negcumsum",
    )(lf)


def _fox_prompt_body(q_ref, k_ref, v_ref, nc_ref, o_ref, m_scr, l_scr, acc_scr, *, tq, rq):
    qi = pl.program_id(1)
    nrc = tq // rq
    m_scr[...] = jnp.full(m_scr.shape, NEG, F32)
    l_scr[...] = jnp.zeros(l_scr.shape, F32)
    acc_scr[...] = jnp.zeros(acc_scr.shape, F32)

    def update(rc, z, v):
        rows = slice(rc * rq, (rc + 1) * rq)
        m_old = m_scr[rows, :]
        m_new = jnp.maximum(m_old, jnp.max(z, axis=-1, keepdims=True))
        alpha = jnp.exp2(m_old - m_new)
        pr = jnp.exp2(z - m_new)
        l_scr[rows, :] = alpha * l_scr[rows, :] + jnp.sum(pr, axis=-1, keepdims=True)
        acc_scr[rows, :] = alpha * acc_scr[rows, :] + _dot(pr.astype(BF16), v)
        m_scr[rows, :] = m_new

    def below_diagonal(ki, carry):
        off = pl.multiple_of(ki * tq, tq)
        k = k_ref[pl.ds(off, tq), :]
        v = v_ref[pl.ds(off, tq), :]
        nc = nc_ref[:, pl.ds(off, tq)]
        for rc in range(nrc):
            update(rc, _dot_nt(q_ref[rc * rq:(rc + 1) * rq, :], k) + nc, v)
        return carry

    lax.fori_loop(0, qi, below_diagonal, 0)

    off = pl.multiple_of(qi * tq, tq)
    for rc in range(nrc):
        n = (rc + 1) * rq
        z = _dot_nt(q_ref[rc * rq:(rc + 1) * rq, :], k_ref[pl.ds(off, n), :]) + nc_ref[:, pl.ds(off, n)]
        row = rc * rq + lax.broadcasted_iota(jnp.int32, (rq, n), 0)
        col = lax.broadcasted_iota(jnp.int32, (rq, n), 1)
        update(rc, jnp.where(row >= col, z, NEG), v_ref[pl.ds(off, n), :])
    o_ref[...] = (acc_scr[...] / l_scr[...]).astype(o_ref.dtype)


def _fox_prompt(qb, kb, vb, negc, tq=1024, rq=256):
    s = qb.shape[0]
    head = lambda h, i: (0, h)
    return pl.pallas_call(
        functools.partial(_fox_prompt_body, tq=tq, rq=rq),
        grid=(N_HEADS, s // tq),
        in_specs=[pl.BlockSpec((tq, HEAD_DIM), lambda h, i: (i, h)),
                  pl.BlockSpec((s, HEAD_DIM), head),
                  pl.BlockSpec((s, HEAD_DIM), head),
                  pl.BlockSpec((None, 1, s), lambda h, i: (h, 0, 0))],
        out_specs=pl.BlockSpec((tq, HEAD_DIM), lambda h, i: (i, h)),
        out_shape=jax.ShapeDtypeStruct((s, ATT_W), BF16),
        scratch_shapes=[pltpu.VMEM((tq, 1), F32), pltpu.VMEM((tq, 1), F32), pltpu.VMEM((tq, HEAD_DIM), F32)],
        compiler_params=_cparams(2),
        name="fox_prompt",
    )(qb, kb, vb, negc.reshape(N_HEADS, 1, s))


def _page_negcumsum(lf, run, lane, rowj):
    x = _lane_cumsum(lf, (16, 32, 64), lane)
    t = jnp.where(lane >= LANES - N_HEADS, x, 0.0)
    for sh in (16, 32, 64):
        t = t + pltpu.roll(t, sh, axis=1)
    e = t
    for sh in (1, 2, 4, 8):
        e = e + jnp.where(rowj >= sh, pltpu.roll(e, sh, axis=0), 0.0)
    c = x + (e - t) + run
    return -c, run + e[N_HEADS - 1:N_HEADS, :]


def _fox_sample_body(pt_ref, q_ref, *refs, npg):
    del pt_ref
    k_refs, v_refs, lf_refs = refs[:npg], refs[npg:2 * npg], refs[2 * npg:3 * npg]
    kn_ref, vn_ref, lfn_ref, o_ref, m_scr, l_scr, acc_scr, run_scr = refs[3 * npg:]
    g = pl.program_id(1)
    rows = N_HEADS * SUBLANES
    tiles = k_refs[0].shape[0] // LANES

    @pl.when(g == 0)
    def _():
        m_scr[...] = jnp.full(m_scr.shape, NEG, F32)
        l_scr[...] = jnp.zeros(l_scr.shape, F32)
        acc_scr[...] = jnp.zeros(acc_scr.shape, F32)
        run_scr[...] = jnp.zeros(run_scr.shape, F32)

    lane = lax.broadcasted_iota(jnp.int32, (rows, LANES), 1)
    row = lax.broadcasted_iota(jnp.int32, (rows, LANES), 0)
    same_head = (lane & (N_HEADS - 1)) == (row >> 3)
    lane16 = lax.broadcasted_iota(jnp.int32, (N_HEADS, LANES), 1)
    row16 = lax.broadcasted_iota(jnp.int32, (N_HEADS, LANES), 0)
    q = q_ref[...]

    def attend(zs, vbs, m, l, acc):
        zmax = zs[0]
        for z in zs[1:]:
            zmax = jnp.maximum(zmax, z)
        m_new = jnp.maximum(m, jnp.max(zmax, axis=-1, keepdims=True))
        alpha = jnp.exp2(m - m_new)
        per = len(zs) // len(vbs)
        psum = None
        pv = None
        for i, vb in enumerate(vbs):
            ps = [jnp.exp2(z - m_new) for z in zs[i * per:(i + 1) * per]]
            for pp in ps:
                psum = pp if psum is None else psum + pp
            pm = ps[0] if per == 1 else jnp.concatenate(ps, axis=1)
            d = _dot(pm.astype(BF16), vb)
            pv = d if pv is None else pv + d
        l = alpha * l + jnp.sum(psum, axis=-1, keepdims=True)
        return m_new, l, alpha * acc + pv

    run = run_scr[...]
    zs = []
    for p in range(npg):
        gt = _dot_nt(q, k_refs[p][...].astype(BF16))
        negc, run = _page_negcumsum(lf_refs[p][...], run, lane16, row16)
        negc = negc * LOG2E
        zs += [jnp.where(same_head, gt[:, t * LANES:(t + 1) * LANES] + negc[t:t + 1, :], NEG)
               for t in range(tiles)]
    m, l, acc = attend(zs, [v_refs[p][...].astype(BF16) for p in range(npg)],
                       m_scr[...], l_scr[...], acc_scr[...])
    m_scr[...], l_scr[...], acc_scr[...], run_scr[...] = m, l, acc, run

    @pl.when(g == pl.num_programs(1) - 1)
    def _():
        cn = _lane_cumsum(lfn_ref[...], (16, 32, 64), lane16[:1]) + run
        zn = _dot_nt(q, kn_ref[...]) - cn * LOG2E
        keep = jnp.logical_and(same_head, (lane >> 4) <= (row & (SUBLANES - 1)))
        _, l2, acc2 = attend([jnp.where(keep, zn, NEG)], [vn_ref[...]], m, l, acc)
        o_ref[...] = acc2 / l2


def _fox_sample(page_table, qcat, ck, cv, clf, kn, vn, lfn):
    nb = qcat.shape[0]
    n_pages = page_table.shape[1]
    npg = PAGES_PER_STEP
    rows = N_HEADS * SUBLANES
    prow = ck.shape[1]

    def page(p):
        return lambda b, g, pt: (pt[b * n_pages + g * npg + p], 0, 0)

    per_b = lambda b, g, pt: (b, 0, 0)
    in_specs = ([pl.BlockSpec((None, rows, HEAD_DIM), per_b)]
                + [pl.BlockSpec((None, prow, HEAD_DIM), page(p)) for p in range(npg)]
                + [pl.BlockSpec((None, prow, HEAD_DIM), page(p)) for p in range(npg)]
                + [pl.BlockSpec((None, N_HEADS, LANES), page(p)) for p in range(npg)]
                + [pl.BlockSpec((None, rows, HEAD_DIM), per_b),
                   pl.BlockSpec((None, rows, HEAD_DIM), per_b),
                   pl.BlockSpec((None, 1, LANES), per_b)])
    grid_spec = pltpu.PrefetchScalarGridSpec(
        num_scalar_prefetch=1,
        grid=(nb, n_pages // npg),
        in_specs=in_specs,
        out_specs=pl.BlockSpec((None, rows, HEAD_DIM), per_b),
        scratch_shapes=[pltpu.VMEM((rows, 1), F32), pltpu.VMEM((rows, 1), F32),
                        pltpu.VMEM((rows, HEAD_DIM), F32), pltpu.VMEM((1, LANES), F32)],
    )
    return pl.pallas_call(
        functools.partial(_fox_sample_body, npg=npg),
        grid_spec=grid_spec,
        out_shape=jax.ShapeDtypeStruct((nb, rows, HEAD_DIM), F32),
        compiler_params=_cparams(2),
        name="fox_sample",
    )(page_table.reshape(-1), qcat, *([ck] * npg), *([cv] * npg), *([clf] * npg), kn, vn, lfn)


def _gmlp_body(gu_ref, gv_ref, gg_ref, w_ref, bt_ref, yb_ref, *vn_ref, blk):
    r = gv_ref.shape[0]
    vn = _rms_rows(gv_ref[...], gg_ref[...])
    if vn_ref:
        vn_ref[0][...] = vn
    vnb = vn.astype(BF16)
    rr = lax.broadcasted_iota(jnp.int32, (r, r), 0)
    cc = lax.broadcasted_iota(jnp.int32, (r, r), 1)
    sh = blk.bit_length() - 1
    keep = jnp.logical_and((rr >> sh) == (cc >> sh), (cc & (blk - 1)) <= (rr & (blk - 1)))
    for g in range(GM_GROUPS):
        sl = slice(g * GM_GROUP_W, (g + 1) * GM_GROUP_W)
        w = jnp.where(keep, w_ref[g], jnp.zeros((), BF16))
        mixed = _dot(w, vnb[:, sl]) + bt_ref[:, g:g + 1]
        yb_ref[:, sl] = (gu_ref[:, sl].astype(F32) * mixed).astype(BF16)


def _gmlp(gu, gv, gg, w, bt, rows, blk, want_vn):
    m = gu.shape[0]
    row_spec = pl.BlockSpec((rows, GM_W), lambda i: (i, 0))
    out_shape = [jax.ShapeDtypeStruct((m, GM_W), BF16)]
    out_specs = [row_spec]
    if want_vn:
        out_shape.append(jax.ShapeDtypeStruct((m, GM_W), F32))
        out_specs.append(row_spec)
    return pl.pallas_call(
        functools.partial(_gmlp_body, blk=blk),
        grid=(m // rows,),
        in_specs=[row_spec, row_spec,
                  pl.BlockSpec((1, GM_W), lambda i: (0, 0)),
                  pl.BlockSpec((GM_GROUPS, rows, rows), lambda i: (0, 0, 0)),
                  pl.BlockSpec((rows, GM_GROUPS), lambda i: (0, 0))],
        out_specs=out_specs,
        out_shape=out_shape,
        compiler_params=_cparams(1),
        name="gmlp",
    )(gu, gv, gg, w, bt)


def _merge_body(ya_ref, yb_ref, woa_ref, wob_ref, sa_ref, sb_ref, m_ref):
    a = _dot(ya_ref[...], woa_ref[...])
    b = _dot(yb_ref[...], wob_ref[...])
    m_ref[...] = (sa_ref[...].astype(F32) * a + sb_ref[...].astype(F32) * b).astype(BF16)


def _merge(ya, yb, woa, wob, sa, sb, tm):
    m, ka = ya.shape
    nt, _, tn = woa.shape
    n = nt * tn
    wtile = pl.BlockSpec((None, ka, tn), lambda i, j: (j, 0, 0))
    return pl.pallas_call(
        _merge_body,
        grid=(m // tm, nt),
        in_specs=[pl.BlockSpec((tm, ka), lambda i, j: (i, 0)),
                  pl.BlockSpec((tm, ka), lambda i, j: (i, 0)),
                  wtile,
                  wtile,
                  pl.BlockSpec((tm, tn), lambda i, j: (i, j)),
                  pl.BlockSpec((tm, tn), lambda i, j: (i, j))],
        out_specs=pl.BlockSpec((tm, tn), lambda i, j: (i, j)),
        out_shape=jax.ShapeDtypeStruct((m, n), BF16),
        compiler_params=_cparams(2),
        name="merge",
    )(ya, yb, woa, wob, sa, sb)


def _resid_mm_body(a_ref, w_ref, x_ref, o_ref):
    o_ref[...] = x_ref[...] + _dot(a_ref[...], w_ref[...])


def _resid_mm(a, w, x, tm, name):
    m, k = a.shape
    nt, _, tn = w.shape
    n = nt * tn
    return pl.pallas_call(
        _resid_mm_body,
        grid=(m // tm, nt),
        in_specs=[pl.BlockSpec((tm, k), lambda i, j: (i, 0)),
                  pl.BlockSpec((None, k, tn), lambda i, j: (j, 0, 0)),
                  pl.BlockSpec((tm, tn), lambda i, j: (i, j))],
        out_specs=pl.BlockSpec((tm, tn), lambda i, j: (i, j)),
        out_shape=jax.ShapeDtypeStruct((m, n), F32),
        compiler_params=_cparams(2),
        name=name,
    )(a, w, x)


def _ffn_up_body(x_ref, g2_ref, wu_ref, wg_ref, cw_ref, cb_ref, *refs, seq_rows):
    i = pl.program_id(0)
    j = pl.program_id(1)
    if seq_rows is None:
        act_ref, st_ref, h_scr, carry_scr = refs
    else:
        prev_ref, act_ref, st_ref, h_scr = refs

    @pl.when(j == 0)
    def _():
        h_scr[...] = _rms_rows(x_ref[...], g2_ref[...]).astype(BF16)

    ch = h_scr.shape[0] // ROW_CHAINS
    wu = wu_ref[...]
    wg = wg_ref[...]
    row = lax.broadcasted_iota(jnp.int32, (ch, wu.shape[1]), 0)
    if seq_rows is None:
        @pl.when(i == 0)
        def _():
            carry_scr[j] = jnp.zeros(carry_scr.shape[1:], F32)

        above = carry_scr[j]
    for ci in range(ROW_CHAINS):
        rows = slice(ci * ch, (ci + 1) * ch)
        h = h_scr[rows, :]
        a = _dot(h, wu)
        gate = _dot(h, wg)
        if seq_rows is None:
            p1 = above[SUBLANES - 1:SUBLANES, :]
            p2 = above[SUBLANES - 2:SUBLANES - 1, :]
            a1 = jnp.where(row == 0, p1, pltpu.roll(a, 1, axis=0))
            a2 = jnp.where(row == 0, p2, jnp.where(row == 1, p1, pltpu.roll(a, 2, axis=0)))
            above = a[ch - SUBLANES:, :]
        else:
            t = row & (seq_rows - 1)
            prev = prev_ref[rows, :]
            a1 = jnp.where(t == 0, pltpu.roll(prev, ch - 1, axis=0), pltpu.roll(a, 1, axis=0))
            a2 = jnp.where(t < 2, prev, pltpu.roll(a, 2, axis=0))
            st_ref[rows, :] = a
        c = cb_ref[...] + cw_ref[0:1, :] * a2 + cw_ref[1:2, :] * a1 + cw_ref[2:3, :] * a
        act_ref[rows, :] = (c * _sigmoid(c) * gate).astype(BF16)
    if seq_rows is None:
        carry_scr[j] = above
        st_ref[...] = above


def _ffn_up(x, g2, wu, wg, cw, cb, tm, prev=None, seq_rows=None):
    m, d = x.shape
    nj, _, tn = wu.shape
    f = nj * tn
    col = lambda i, j: (0, j)
    wtile = pl.BlockSpec((None, d, tn), lambda i, j: (j, 0, 0))
    x_mode = {"pipeline_mode": pl.Buffered(1)} if tm * d * 4 > VMEM_LIMIT // 4 else {}
    in_specs = [pl.BlockSpec((tm, d), lambda i, j: (i, 0), **x_mode),
                pl.BlockSpec((1, d), lambda i, j: (0, 0)),
                wtile,
                wtile,
                pl.BlockSpec((CONV_W, tn), col),
                pl.BlockSpec((1, tn), col)]
    args = [x, g2, wu, wg, cw, cb]
    scratch = [pltpu.VMEM((tm, d), BF16)]
    if seq_rows is None:
        st_shape = jax.ShapeDtypeStruct((m // tm * SUBLANES, f), F32)
        st_spec = pl.BlockSpec((SUBLANES, tn), lambda i, j: (i, j))
        scratch.append(pltpu.VMEM((nj, SUBLANES, tn), F32))
    else:
        in_specs.append(pl.BlockSpec((tm, tn), lambda i, j: (i, j)))
        args.append(prev)
        st_shape = jax.ShapeDtypeStruct((m, f), F32)
        st_spec = pl.BlockSpec((tm, tn), lambda i, j: (i, j))
    return pl.pallas_call(
        functools.partial(_ffn_up_body, seq_rows=seq_rows),
        grid=(m // tm, nj),
        in_specs=in_specs,
        out_specs=[pl.BlockSpec((tm, tn), lambda i, j: (i, j)), st_spec],
        out_shape=[jax.ShapeDtypeStruct((m, f), BF16), st_shape],
        scratch_shapes=scratch,
        compiler_params=_cparams(2),
        name="ffn_up",
    )(*args)


def _layer_dense_tail(x, ya, yb, sa, sb, wts, tm, tm_down, prev=None, seq_rows=None):
    mrg = _merge(ya, yb, wts["oa"], wts["ob"], sa, sb, tm)
    x1 = _resid_mm(mrg, wts["out"], x, tm, "out_proj")
    act, st = _ffn_up(x1, wts["g2"], wts["up"], wts["gate"], wts["cw"], wts["cb"], tm, prev, seq_rows)
    y = _resid_mm(act, wts["down"], x1, tm_down, "ffn_down")
    return y, st


def kernel(x_prompt, x_sample, cache_k, cache_v, cache_logf, state_conv, page_table, norm1_g, w_in, b_f,
           q_norm_g, k_norm_g, gm_norm_g, w_s, b_s, w_oa, w_ob, w_out, norm2_g, w_up, w_gate, conv_w, conv_b,
           w_down):
    depth = w_in.shape[0]
    assert depth == 1
    bp, sp, d = x_prompt.shape
    assert bp == 1
    bs, ts, _ = x_sample.shape
    assert ts == SUBLANES
    f = w_up.shape[-1]
    l = 0

    qkv_w = 3 * ATT_W
    wi = w_in[l]
    wq = _tile_major(wi[:, :qkv_w].astype(BF16), PROJ_TN)
    wr = _tile_major(wi[:, qkv_w + N_HEADS:].astype(BF16), PROJ_TN)
    wf = jnp.pad(wi[:, qkv_w:qkv_w + N_HEADS], ((0, 0), (0, LANES - N_HEADS))).astype(BF16)
    bfp = jnp.pad(b_f[l], (0, LANES - N_HEADS)).reshape(1, LANES)
    g1 = norm1_g[l].reshape(1, d)
    qg = q_norm_g[l].reshape(1, HEAD_DIM)
    kg = k_norm_g[l].reshape(1, HEAD_DIM)
    gg = gm_norm_g[l].reshape(1, GM_W)
    tiled = lambda w, tn: _tile_major(w[l].astype(BF16), tn)
    wts = {"oa": tiled(w_oa, MERGE_TN), "ob": tiled(w_ob, MERGE_TN), "out": tiled(w_out, MERGE_TN),
           "g2": norm2_g[l].reshape(1, d), "up": tiled(w_up, FFN_TN), "gate": tiled(w_gate, FFN_TN),
           "cw": conv_w[l], "cb": conv_b[l].reshape(1, f), "down": tiled(w_down, FFN_TN)}
    ws = w_s[l].astype(BF16)

    xp = x_prompt.reshape(sp, d)
    q, kf, kb, vf, vb, gu, gv, sa, sb, lf = _inproj(xp, g1, wq, wr, wf, bfp, qg, kg, tm=512)
    ya = _fox_prompt(q, kb, vb, _negcumsum(lf))
    (yb,) = _gmlp(gu, gv, gg, ws, jnp.transpose(b_s[l]), CHUNK, CHUNK, False)
    yp, st_p = _layer_dense_tail(xp, ya, yb, sa, sb, wts, 1024, 512)

    ms = bs * ts
    xs = x_sample.reshape(ms, d)
    q, kf_s, kb_s, vf_s, vb_s, gu, gv, sa, sb, lf_s = _inproj(xs, g1, wq, wr, wf, bfp, qg, kg, tm=ms)
    rows = N_HEADS * ts
    qcat = jnp.transpose(q.reshape(bs, ts, N_HEADS, HEAD_DIM), (0, 2, 1, 3)).reshape(bs, rows, HEAD_DIM)
    n_pool = cache_k.shape[1]
    page_rows = cache_k.shape[2] * N_HEADS
    ck = cache_k[l].reshape(n_pool, page_rows, HEAD_DIM)
    cv = cache_v[l].reshape(n_pool, page_rows, HEAD_DIM)
    clf = cache_logf[l].reshape(n_pool, page_rows // LANES, LANES)
    o = _fox_sample(page_table, qcat, ck, cv, clf,
                    kb_s.reshape(bs, rows, HEAD_DIM), vb_s.reshape(bs, rows, HEAD_DIM),
                    lf_s[:, :N_HEADS].reshape(bs, 1, rows))
    ya = jnp.transpose(o.reshape(bs, N_HEADS, ts, HEAD_DIM), (0, 2, 1, 3)).reshape(ms, ATT_W).astype(BF16)
    w_small = jnp.tile(ws[:, :ts, :ts], (1, bs, bs))
    bt_small = jnp.tile(jnp.transpose(b_s[l][:, :ts]), (bs, 1))
    yb, vn = _gmlp(gu, gv, gg, w_small, bt_small, ms, ts, True)
    prev = jnp.pad(state_conv[l], ((0, 0), (0, ts - (CONV_W - 1)), (0, 0))).reshape(ms, f)
    ys, a_s = _layer_dense_tail(xs, ya, yb, sa, sb, wts, ms, ms, prev, ts)

    return (yp.reshape(1, sp, d),
            ys.reshape(bs, ts, d),
            kf.reshape(1, 1, sp, N_HEADS, HEAD_DIM),
            vf.reshape(1, 1, sp, N_HEADS, HEAD_DIM),
            lf[:, :N_HEADS].reshape(1, 1, sp, N_HEADS),
            st_p[st_p.shape[0] - (CONV_W - 1):].reshape(1, 1, CONV_W - 1, f),
            kf_s.reshape(1, bs, ts, N_HEADS, HEAD_DIM),
            vf_s.reshape(1, bs, ts, N_HEADS, HEAD_DIM),
            lf_s[:, :N_HEADS].reshape(1, bs, ts, N_HEADS),
            vn.reshape(1, bs, ts, GM_W),
            a_s.reshape(bs, ts, f)[:, ts - (CONV_W - 1):].reshape(1, bs, CONV_W - 1, f))
```

```python
import functools

import jax
import jax.numpy as jnp
import numpy as np
from jax import lax
from jax.experimental import pallas as pl
from jax.experimental.pallas import tpu as pltpu

F32 = jnp.float32
BF16 = jnp.bfloat16

LANES = 128
SUBLANES = 8
VMEM_LIMIT = 56 * 1024 * 1024

HEAD_DIM = 128
N_HEADS = 16
ATT_W = N_HEADS * HEAD_DIM
GM_GROUPS = 16
GM_GROUP_W = 128
GM_W = GM_GROUPS * GM_GROUP_W
CHUNK = 128
CONV_W = 3
EPS = 1e-6
NEG = -1e30
SCALE = HEAD_DIM ** -0.5
LOG2E = float(np.log2(np.e))
Q_FOLD = SCALE * LOG2E

PROJ_TN = 512
MERGE_TN = 512
FFN_TN = 256
PAGES_PER_STEP = 8


def _cparams(n_axes):
    return pltpu.CompilerParams(dimension_semantics=("arbitrary",) * n_axes, vmem_limit_bytes=VMEM_LIMIT)


def _dot(a, b):
    return jnp.dot(a, b, preferred_element_type=F32)


def _dot_nt(a, b):
    return lax.dot_general(a, b, (((1,), (1,)), ((), ())), preferred_element_type=F32)


def _rms_rows(x, g):
    return x * lax.rsqrt(jnp.mean(x * x, axis=-1, keepdims=True) + EPS) * g


def _sigmoid(x):
    return 1.0 / (1.0 + jnp.exp(-x))


def _gelu_tanh(x):
    return 0.5 * x * (1.0 + jnp.tanh(np.sqrt(2.0 / np.pi).astype(np.float32) * (x + 0.044715 * (x * x * x))))


def _log_sigmoid(x):
    return jnp.minimum(x, 0.0) - jnp.log(1.0 + jnp.exp(-jnp.abs(x)))


_SEC = {"q": (0, 4), "k": (4, 4), "v": (8, 4), "u": (12, 4), "vg": (16, 4), "ga": (20, 8), "gb": (28, 8)}


_N_QKV = 12
ROW_CHAINS = 1


def _inproj_body(x_ref, g1_ref, wq_ref, wr_ref, wf_ref, bf_ref, qg_ref, kg_ref,
                 q_ref, kf_ref, kb_ref, vf_ref, vb_ref, gu_ref, gv_ref, sa_ref, sb_ref, lf_ref, h_scr):
    j = pl.program_id(1)

    @pl.when(j == 0)
    def _():
        hb = _rms_rows(x_ref[...], g1_ref[...]).astype(BF16)
        h_scr[...] = hb
        lf_ref[...] = _log_sigmoid(_dot(hb, wf_ref[...]) + bf_ref[...])

    ch = h_scr.shape[0] // ROW_CHAINS

    def section(name, w_ref, epilogue):
        lo, n = _SEC[name]

        @pl.when(jnp.logical_and(j >= lo, j < lo + n))
        def _():
            w = w_ref[...]
            for c in range(ROW_CHAINS):
                rows = slice(c * ch, (c + 1) * ch)
                epilogue(rows, _dot(h_scr[rows, :], w))

    heads = [slice(h * HEAD_DIM, (h + 1) * HEAD_DIM) for h in range(PROJ_TN // HEAD_DIM)]

    def q_epi(rows, acc):
        for sl in heads:
            q_ref[rows, sl] = (_rms_rows(acc[:, sl], qg_ref[...]) * Q_FOLD).astype(BF16)

    def k_epi(rows, acc):
        for sl in heads:
            y = _rms_rows(acc[:, sl], kg_ref[...])
            kf_ref[rows, sl] = y
            kb_ref[rows, sl] = y.astype(BF16)

    def v_epi(rows, acc):
        vf_ref[rows, :] = acc
        vb_ref[rows, :] = acc.astype(BF16)

    def u_epi(rows, acc):
        gu_ref[rows, :] = _gelu_tanh(acc).astype(BF16)

    def vg_epi(rows, acc):
        gv_ref[rows, :] = _gelu_tanh(acc)

    def ga_epi(rows, acc):
        sa_ref[rows, :] = _sigmoid(acc).astype(BF16)

    def gb_epi(rows, acc):
        sb_ref[rows, :] = _sigmoid(acc).astype(BF16)

    section("q", wq_ref, q_epi)
    section("k", wq_ref, k_epi)
    section("v", wq_ref, v_epi)
    section("u", wr_ref, u_epi)
    section("vg", wr_ref, vg_epi)
    section("ga", wr_ref, ga_epi)
    section("gb", wr_ref, gb_epi)


def _inproj(x, g1, wq, wr, wf, bfp, qg, kg, tm):
    m, d = x.shape
    nq = wq.shape[1] // PROJ_TN
    nr = wr.shape[1] // PROJ_TN
    assert nq == _N_QKV

    def sec(name):
        lo, n = _SEC[name]
        return pl.BlockSpec((tm, PROJ_TN), lambda i, j: (i, jnp.clip(j - lo, 0, n - 1)))

    def shp(name, dt):
        return jax.ShapeDtypeStruct((m, _SEC[name][1] * PROJ_TN), dt)

    const = lambda i, j: (0, 0)
    return pl.pallas_call(
        _inproj_body,
        grid=(m // tm, nq + nr),
        in_specs=[pl.BlockSpec((tm, d), lambda i, j: (i, 0)),
                  pl.BlockSpec((1, d), const),
                  pl.BlockSpec((d, PROJ_TN), lambda i, j: (0, jnp.minimum(j, nq - 1))),
                  pl.BlockSpec((d, PROJ_TN), lambda i, j: (0, jnp.maximum(j - nq, 0))),
                  pl.BlockSpec((d, LANES), const),
                  pl.BlockSpec((1, LANES), const),
                  pl.BlockSpec((1, HEAD_DIM), const),
                  pl.BlockSpec((1, HEAD_DIM), const)],
        out_specs=[sec("q"), sec("k"), sec("k"), sec("v"), sec("v"), sec("u"), sec("vg"), sec("ga"), sec("gb"),
                   pl.BlockSpec((tm, LANES), lambda i, j: (i, 0))],
        out_shape=[shp("q", BF16), shp("k", F32), shp("k", BF16), shp("v", F32), shp("v", BF16),
                   shp("u", BF16), shp("vg", F32), shp("ga", BF16), shp("gb", BF16),
                   jax.ShapeDtypeStruct((m, LANES), F32)],
        scratch_shapes=[pltpu.VMEM((tm, d), BF16)],
        compiler_params=_cparams(2),
        name="inproj",
    )(x, g1, wq, wr, wf, bfp, qg, kg)


def _lane_cumsum(x, shifts, lane):
    for sh in shifts:
        x = x + jnp.where(lane >= sh, pltpu.roll(x, sh, axis=1), 0.0)
    return x


def _negcumsum_body(lf_ref, o_ref):
    nblk = lf_ref.shape[0] // LANES
    lane = lax.broadcasted_iota(jnp.int32, (N_HEADS, LANES), 1)

    def body(b, carry):
        off = pl.multiple_of(b * LANES, LANES)
        xt = jnp.transpose(lf_ref[pl.ds(off, LANES), :])[:N_HEADS, :]
        c = _lane_cumsum(xt, (1, 2, 4, 8, 16, 32, 64), lane) + carry
        o_ref[:, pl.ds(off, LANES)] = c * (-LOG2E)
        return c[:, LANES - 1:LANES]

    lax.fori_loop(0, nblk, body, jnp.zeros((N_HEADS, 1), F32))


def _negcumsum(lf):
    s = lf.shape[0]
    return pl.pallas_call(
        _negcumsum_body,
        out_shape=jax.ShapeDtypeStruct((N_HEADS, s), F32),
        compiler_params=pltpu.CompilerParams(vmem_limit_bytes=VMEM_LIMIT),
        name="negcumsum",
    )(lf)


def _fox_prompt_body(q_ref, k_ref, v_ref, nc_ref, o_ref, m_scr, l_scr, acc_scr, *, tq, rq):
    qi = pl.program_id(1)
    nrc = tq // rq
    m_scr[...] = jnp.full(m_scr.shape, NEG, F32)
    l_scr[...] = jnp.zeros(l_scr.shape, F32)
    acc_scr[...] = jnp.zeros(acc_scr.shape, F32)

    def update(rc, zs, vs):
        rows = slice(rc * rq, (rc + 1) * rq)
        m_old = m_scr[rows, :]
        m_new = m_old
        for z in zs:
            m_new = jnp.maximum(m_new, jnp.max(z, axis=-1, keepdims=True))
        alpha = jnp.exp2(m_old - m_new)
        l_new = alpha * l_scr[rows, :]
        acc = alpha * acc_scr[rows, :]
        for z, v in zip(zs, vs):
            pr = jnp.exp2(z - m_new)
            l_new = l_new + jnp.sum(pr, axis=-1, keepdims=True)
            acc = acc + _dot(pr.astype(BF16), v)
        l_scr[rows, :] = l_new
        acc_scr[rows, :] = acc
        m_scr[rows, :] = m_new

    def scores(rc, start, n):
        return (_dot_nt(q_ref[rc * rq:(rc + 1) * rq, :], k_ref[pl.ds(start, n), :]) + nc_ref[:, pl.ds(start, n)])

    def below_diagonal(ki, carry):
        off = pl.multiple_of(ki * tq, tq)
        v = v_ref[pl.ds(off, tq), :]
        for rc in range(nrc):
            update(rc, [scores(rc, off, tq)], [v])
        return carry

    lax.fori_loop(0, qi, below_diagonal, 0)

    off = pl.multiple_of(qi * tq, tq)
    causal = (lax.broadcasted_iota(jnp.int32, (rq, rq), 0) >= lax.broadcasted_iota(jnp.int32, (rq, rq), 1))
    for rc in range(nrc):
        n = rc * rq
        dstart = pl.multiple_of(off + n, rq)
        zs = [jnp.where(causal, scores(rc, dstart, rq), NEG)]
        vs = [v_ref[pl.ds(dstart, rq), :]]
        if n:
            zs.append(scores(rc, off, n))
            vs.append(v_ref[pl.ds(off, n), :])
        update(rc, zs, vs)
    o_ref[...] = (acc_scr[...] / l_scr[...]).astype(o_ref.dtype)


def _fox_prompt(qb, kb, vb, negc, tq=2048, rq=256):
    s = qb.shape[0]
    head = lambda h, i: (0, h)
    return pl.pallas_call(
        functools.partial(_fox_prompt_body, tq=tq, rq=rq),
        grid=(N_HEADS, s // tq),
        in_specs=[pl.BlockSpec((tq, HEAD_DIM), lambda h, i: (i, h)),
                  pl.BlockSpec((s, HEAD_DIM), head),
                  pl.BlockSpec((s, HEAD_DIM), head),
                  pl.BlockSpec((None, 1, s), lambda h, i: (h, 0, 0))],
        out_specs=pl.BlockSpec((tq, HEAD_DIM), lambda h, i: (i, h)),
        out_shape=jax.ShapeDtypeStruct((s, ATT_W), BF16),
        scratch_shapes=[pltpu.VMEM((tq, 1), F32), pltpu.VMEM((tq, 1), F32), pltpu.VMEM((tq, HEAD_DIM), F32)],
        compiler_params=_cparams(2),
        name="fox_prompt",
    )(qb, kb, vb, negc.reshape(N_HEADS, 1, s))


def _page_negcumsum(lf, run, lane, rowj):
    x = _lane_cumsum(lf, (16, 32, 64), lane)
    t = jnp.where(lane >= LANES - N_HEADS, x, 0.0)
    for sh in (16, 32, 64):
        t = t + pltpu.roll(t, sh, axis=1)
    e = t
    for sh in (1, 2, 4, 8):
        e = e + jnp.where(rowj >= sh, pltpu.roll(e, sh, axis=0), 0.0)
    c = x + (e - t) + run
    return -c, run + e[N_HEADS - 1:N_HEADS, :]


def _fox_sample_body(pt_ref, q_ref, *refs, npg):
    del pt_ref
    k_refs, v_refs, lf_refs = refs[:npg], refs[npg:2 * npg], refs[2 * npg:3 * npg]
    kn_ref, vn_ref, lfn_ref, o_ref, m_scr, l_scr, acc_scr, run_scr = refs[3 * npg:]
    g = pl.program_id(1)
    rows = N_HEADS * SUBLANES
    tiles = k_refs[0].shape[0] // LANES

    @pl.when(g == 0)
    def _():
        m_scr[...] = jnp.full(m_scr.shape, NEG, F32)
        l_scr[...] = jnp.zeros(l_scr.shape, F32)
        acc_scr[...] = jnp.zeros(acc_scr.shape, F32)
        run_scr[...] = jnp.zeros(run_scr.shape, F32)

    lane = lax.broadcasted_iota(jnp.int32, (rows, LANES), 1)
    row = lax.broadcasted_iota(jnp.int32, (rows, LANES), 0)
    same_head = (lane & (N_HEADS - 1)) == (row >> 3)
    lane16 = lax.broadcasted_iota(jnp.int32, (N_HEADS, LANES), 1)
    row16 = lax.broadcasted_iota(jnp.int32, (N_HEADS, LANES), 0)
    q = q_ref[...]

    def attend(zs, vbs, m, l, acc):
        zmax = zs[0]
        for z in zs[1:]:
            zmax = jnp.maximum(zmax, z)
        m_new = jnp.maximum(m, jnp.max(zmax, axis=-1, keepdims=True))
        alpha = jnp.exp2(m - m_new)
        per = len(zs) // len(vbs)
        psum = None
        pv = None
        for i, vb in enumerate(vbs):
            ps = [jnp.exp2(z - m_new) for z in zs[i * per:(i + 1) * per]]
            for pp in ps:
                psum = pp if psum is None else psum + pp
            pm = ps[0] if per == 1 else jnp.concatenate(ps, axis=1)
            d = _dot(pm.astype(BF16), vb)
            pv = d if pv is None else pv + d
        l = alpha * l + jnp.sum(psum, axis=-1, keepdims=True)
        return m_new, l, alpha * acc + pv

    run = run_scr[...]
    zs = []
    for p in range(npg):
        gt = _dot_nt(q, k_refs[p][...].astype(BF16))
        negc, run = _page_negcumsum(lf_refs[p][...], run, lane16, row16)
        negc = negc * LOG2E
        zs += [jnp.where(same_head, gt[:, t * LANES:(t + 1) * LANES] + negc[t:t + 1, :], NEG)
               for t in range(tiles)]
    m, l, acc = attend(zs, [v_refs[p][...].astype(BF16) for p in range(npg)],
                       m_scr[...], l_scr[...], acc_scr[...])
    m_scr[...], l_scr[...], acc_scr[...], run_scr[...] = m, l, acc, run

    @pl.when(g == pl.num_programs(1) - 1)
    def _():
        cn = _lane_cumsum(lfn_ref[...], (16, 32, 64), lane16[:1]) + run
        zn = _dot_nt(q, kn_ref[...]) - cn * LOG2E
        keep = jnp.logical_and(same_head, (lane >> 4) <= (row & (SUBLANES - 1)))
        _, l2, acc2 = attend([jnp.where(keep, zn, NEG)], [vn_ref[...]], m, l, acc)
        o_ref[...] = acc2 / l2


def _fox_sample(page_table, qcat, ck, cv, clf, kn, vn, lfn):
    nb = qcat.shape[0]
    n_pages = page_table.shape[1]
    npg = PAGES_PER_STEP
    rows = N_HEADS * SUBLANES
    prow = ck.shape[1]

    def page(p):
        return lambda b, g, pt: (pt[b * n_pages + g * npg + p], 0, 0)

    per_b = lambda b, g, pt: (b, 0, 0)
    in_specs = ([pl.BlockSpec((None, rows, HEAD_DIM), per_b)]
                + [pl.BlockSpec((None, prow, HEAD_DIM), page(p)) for p in range(npg)]
                + [pl.BlockSpec((None, prow, HEAD_DIM), page(p)) for p in range(npg)]
                + [pl.BlockSpec((None, N_HEADS, LANES), page(p)) for p in range(npg)]
                + [pl.BlockSpec((None, rows, HEAD_DIM), per_b),
                   pl.BlockSpec((None, rows, HEAD_DIM), per_b),
                   pl.BlockSpec((None, 1, LANES), per_b)])
    grid_spec = pltpu.PrefetchScalarGridSpec(
        num_scalar_prefetch=1,
        grid=(nb, n_pages // npg),
        in_specs=in_specs,
        out_specs=pl.BlockSpec((None, rows, HEAD_DIM), per_b),
        scratch_shapes=[pltpu.VMEM((rows, 1), F32), pltpu.VMEM((rows, 1), F32),
                        pltpu.VMEM((rows, HEAD_DIM), F32), pltpu.VMEM((1, LANES), F32)],
    )
    return pl.pallas_call(
        functools.partial(_fox_sample_body, npg=npg),
        grid_spec=grid_spec,
        out_shape=jax.ShapeDtypeStruct((nb, rows, HEAD_DIM), F32),
        compiler_params=_cparams(2),
        name="fox_sample",
    )(page_table.reshape(-1), qcat, *([ck] * npg), *([cv] * npg), *([clf] * npg), kn, vn, lfn)


def _gmlp_body(gu_ref, gv_ref, gg_ref, w_ref, bt_ref, yb_ref, *vn_ref, blk):
    r = gv_ref.shape[0]
    vn = _rms_rows(gv_ref[...], gg_ref[...])
    if vn_ref:
        vn_ref[0][...] = vn
    vnb = vn.astype(BF16)
    rr = lax.broadcasted_iota(jnp.int32, (r, r), 0)
    cc = lax.broadcasted_iota(jnp.int32, (r, r), 1)
    sh = blk.bit_length() - 1
    keep = jnp.logical_and((rr >> sh) == (cc >> sh), (cc & (blk - 1)) <= (rr & (blk - 1)))
    for g in range(GM_GROUPS):
        sl = slice(g * GM_GROUP_W, (g + 1) * GM_GROUP_W)
        w = jnp.where(keep, w_ref[g], jnp.zeros((), BF16))
        mixed = _dot(w, vnb[:, sl]) + bt_ref[:, g:g + 1]
        yb_ref[:, sl] = (gu_ref[:, sl].astype(F32) * mixed).astype(BF16)


def _gmlp(gu, gv, gg, w, bt, rows, blk, want_vn):
    m = gu.shape[0]
    row_spec = pl.BlockSpec((rows, GM_W), lambda i: (i, 0))
    out_shape = [jax.ShapeDtypeStruct((m, GM_W), BF16)]
    out_specs = [row_spec]
    if want_vn:
        out_shape.append(jax.ShapeDtypeStruct((m, GM_W), F32))
        out_specs.append(row_spec)
    return pl.pallas_call(
        functools.partial(_gmlp_body, blk=blk),
        grid=(m // rows,),
        in_specs=[row_spec, row_spec,
                  pl.BlockSpec((1, GM_W), lambda i: (0, 0)),
                  pl.BlockSpec((GM_GROUPS, rows, rows), lambda i: (0, 0, 0)),
                  pl.BlockSpec((rows, GM_GROUPS), lambda i: (0, 0))],
        out_specs=out_specs,
        out_shape=out_shape,
        compiler_params=_cparams(1),
        name="gmlp",
    )(gu, gv, gg, w, bt)


def _merge_body(ya_ref, yb_ref, woa_ref, wob_ref, sa_ref, sb_ref, m_ref):
    a = _dot(ya_ref[...], woa_ref[...])
    b = _dot(yb_ref[...], wob_ref[...])
    m_ref[...] = (sa_ref[...].astype(F32) * a + sb_ref[...].astype(F32) * b).astype(BF16)


def _merge(ya, yb, woa, wob, sa, sb, tm, tn=MERGE_TN):
    m, ka = ya.shape
    n = woa.shape[1]
    nt = n // tn
    wtile = pl.BlockSpec((ka, tn), lambda i, j: (0, j))
    return pl.pallas_call(
        _merge_body,
        grid=(m // tm, nt),
        in_specs=[pl.BlockSpec((tm, ka), lambda i, j: (i, 0)),
                  pl.BlockSpec((tm, ka), lambda i, j: (i, 0)),
                  wtile,
                  wtile,
                  pl.BlockSpec((tm, tn), lambda i, j: (i, j)),
                  pl.BlockSpec((tm, tn), lambda i, j: (i, j))],
        out_specs=pl.BlockSpec((tm, tn), lambda i, j: (i, j)),
        out_shape=jax.ShapeDtypeStruct((m, n), BF16),
        compiler_params=_cparams(2),
        name="merge",
    )(ya, yb, woa, wob, sa, sb)


def _resid_mm_body(a_ref, w_ref, x_ref, o_ref):
    o_ref[...] = x_ref[...] + _dot(a_ref[...], w_ref[...])


def _resid_mm(a, w, x, tm, tn, name):
    m, k = a.shape
    n = w.shape[1]
    return pl.pallas_call(
        _resid_mm_body,
        grid=(m // tm, n // tn),
        in_specs=[pl.BlockSpec((tm, k), lambda i, j: (i, 0)),
                  pl.BlockSpec((k, tn), lambda i, j: (0, j)),
                  pl.BlockSpec((tm, tn), lambda i, j: (i, j))],
        out_specs=pl.BlockSpec((tm, tn), lambda i, j: (i, j)),
        out_shape=jax.ShapeDtypeStruct((m, n), F32),
        compiler_params=_cparams(2),
        name=name,
    )(a, w, x)


def _ffn_up_body(x_ref, g2_ref, wu_ref, wg_ref, cw_ref, cb_ref, *refs, seq_rows):
    i = pl.program_id(0)
    j = pl.program_id(1)
    if seq_rows is None:
        act_ref, st_ref, h_scr, carry_scr = refs
    else:
        prev_ref, act_ref, st_ref, h_scr = refs

    @pl.when(j == 0)
    def _():
        h_scr[...] = _rms_rows(x_ref[...], g2_ref[...]).astype(BF16)

    ch = h_scr.shape[0] // ROW_CHAINS
    wu = wu_ref[...]
    wg = wg_ref[...]
    row = lax.broadcasted_iota(jnp.int32, (ch, wu.shape[1]), 0)
    if seq_rows is None:
        @pl.when(i == 0)
        def _():
            carry_scr[j] = jnp.zeros(carry_scr.shape[1:], F32)

        above = carry_scr[j]
    for ci in range(ROW_CHAINS):
        rows = slice(ci * ch, (ci + 1) * ch)
        h = h_scr[rows, :]
        a = _dot(h, wu)
        gate = _dot(h, wg)
        if seq_rows is None:
            p1 = above[SUBLANES - 1:SUBLANES, :]
            p2 = above[SUBLANES - 2:SUBLANES - 1, :]
            a1 = jnp.where(row == 0, p1, pltpu.roll(a, 1, axis=0))
            a2 = jnp.where(row == 0, p2, jnp.where(row == 1, p1, pltpu.roll(a, 2, axis=0)))
            above = a[ch - SUBLANES:, :]
        else:
            t = row & (seq_rows - 1)
            prev = prev_ref[rows, :]
            a1 = jnp.where(t == 0, pltpu.roll(prev, ch - 1, axis=0), pltpu.roll(a, 1, axis=0))
            a2 = jnp.where(t < 2, prev, pltpu.roll(a, 2, axis=0))
            st_ref[rows, :] = a
        c = cb_ref[...] + cw_ref[0:1, :] * a2 + cw_ref[1:2, :] * a1 + cw_ref[2:3, :] * a
        act_ref[rows, :] = (c * _sigmoid(c) * gate).astype(BF16)
    if seq_rows is None:
        carry_scr[j] = above
        st_ref[...] = above


def _ffn_up(x, g2, wu, wg, cw, cb, tm, prev=None, seq_rows=None, tn=FFN_TN):
    m, d = x.shape
    f = wu.shape[1]
    nj = f // tn
    col = lambda i, j: (0, j)
    wtile = pl.BlockSpec((d, tn), col)
    x_mode = {"pipeline_mode": pl.Buffered(1)} if tm * d * 4 > VMEM_LIMIT // 4 else {}
    in_specs = [pl.BlockSpec((tm, d), lambda i, j: (i, 0), **x_mode),
                pl.BlockSpec((1, d), lambda i, j: (0, 0)),
                wtile,
                wtile,
                pl.BlockSpec((CONV_W, tn), col),
                pl.BlockSpec((1, tn), col)]
    args = [x, g2, wu, wg, cw, cb]
    scratch = [pltpu.VMEM((tm, d), BF16)]
    if seq_rows is None:
        st_shape = jax.ShapeDtypeStruct((m // tm * SUBLANES, f), F32)
        st_spec = pl.BlockSpec((SUBLANES, tn), lambda i, j: (i, j))
        scratch.append(pltpu.VMEM((nj, SUBLANES, tn), F32))
    else:
        in_specs.append(pl.BlockSpec((tm, tn), lambda i, j: (i, j)))
        args.append(prev)
        st_shape = jax.ShapeDtypeStruct((m, f), F32)
        st_spec = pl.BlockSpec((tm, tn), lambda i, j: (i, j))
    return pl.pallas_call(
        functools.partial(_ffn_up_body, seq_rows=seq_rows),
        grid=(m // tm, nj),
        in_specs=in_specs,
        out_specs=[pl.BlockSpec((tm, tn), lambda i, j: (i, j)), st_spec],
        out_shape=[jax.ShapeDtypeStruct((m, f), BF16), st_shape],
        scratch_shapes=scratch,
        compiler_params=_cparams(2),
        name="ffn_up",
    )(*args)


def _layer_dense_tail(x, ya, yb, sa, sb, wts, tm, tm_down, prev=None, seq_rows=None):
    mrg = _merge(ya, yb, wts["oa"], wts["ob"], sa, sb, tm)
    x1 = _resid_mm(mrg, wts["out"], x, tm, MERGE_TN, "out_proj")
    act, st = _ffn_up(x1, wts["g2"], wts["up"], wts["gate"], wts["cw"], wts["cb"], tm, prev, seq_rows)
    y = _resid_mm(act, wts["down"], x1, tm_down, FFN_TN, "ffn_down")
    return y, st


def kernel(x_prompt, x_sample, cache_k, cache_v, cache_logf, state_conv, page_table, norm1_g, w_in, b_f,
           q_norm_g, k_norm_g, gm_norm_g, w_s, b_s, w_oa, w_ob, w_out, norm2_g, w_up, w_gate, conv_w, conv_b,
           w_down):
    depth = w_in.shape[0]
    assert depth == 1
    bp, sp, d = x_prompt.shape
    assert bp == 1
    bs, ts, _ = x_sample.shape
    assert ts == SUBLANES
    f = w_up.shape[-1]
    l = 0

    qkv_w = 3 * ATT_W
    wi = w_in[l]
    wq = wi[:, :qkv_w].astype(BF16)
    wr = wi[:, qkv_w + N_HEADS:].astype(BF16)
    wf = jnp.pad(wi[:, qkv_w:qkv_w + N_HEADS], ((0, 0), (0, LANES - N_HEADS))).astype(BF16)
    bfp = jnp.pad(b_f[l], (0, LANES - N_HEADS)).reshape(1, LANES)
    g1 = norm1_g[l].reshape(1, d)
    qg = q_norm_g[l].reshape(1, HEAD_DIM)
    kg = k_norm_g[l].reshape(1, HEAD_DIM)
    gg = gm_norm_g[l].reshape(1, GM_W)
    wts = {"oa": w_oa[l].astype(BF16), "ob": w_ob[l].astype(BF16), "out": w_out[l].astype(BF16),
           "g2": norm2_g[l].reshape(1, d), "up": w_up[l].astype(BF16), "gate": w_gate[l].astype(BF16),
           "cw": conv_w[l], "cb": conv_b[l].reshape(1, f), "down": w_down[l].astype(BF16)}
    ws = w_s[l].astype(BF16)

    xp = x_prompt.reshape(sp, d)
    q, kf, kb, vf, vb, gu, gv, sa, sb, lf = _inproj(xp, g1, wq, wr, wf, bfp, qg, kg, tm=512)
    ya = _fox_prompt(q, kb, vb, _negcumsum(lf))
    (yb,) = _gmlp(gu, gv, gg, ws, jnp.transpose(b_s[l]), CHUNK, CHUNK, False)
    yp, st_p = _layer_dense_tail(xp, ya, yb, sa, sb, wts, 1024, 512)

    ms = bs * ts
    xs = x_sample.reshape(ms, d)
    q, kf_s, kb_s, vf_s, vb_s, gu, gv, sa, sb, lf_s = _inproj(xs, g1, wq, wr, wf, bfp, qg, kg, tm=ms)
    rows = N_HEADS * ts
    qcat = jnp.transpose(q.reshape(bs, ts, N_HEADS, HEAD_DIM), (0, 2, 1, 3)).reshape(bs, rows, HEAD_DIM)
    n_pool = cache_k.shape[1]
    page_rows = cache_k.shape[2] * N_HEADS
    ck = cache_k[l].reshape(n_pool, page_rows, HEAD_DIM)
    cv = cache_v[l].reshape(n_pool, page_rows, HEAD_DIM)
    clf = cache_logf[l].reshape(n_pool, page_rows // LANES, LANES)
    o = _fox_sample(page_table, qcat, ck, cv, clf,
                    kb_s.reshape(bs, rows, HEAD_DIM), vb_s.reshape(bs, rows, HEAD_DIM),
                    lf_s[:, :N_HEADS].reshape(bs, 1, rows))
    ya = jnp.transpose(o.reshape(bs, N_HEADS, ts, HEAD_DIM), (0, 2, 1, 3)).reshape(ms, ATT_W).astype(BF16)
    w_small = jnp.tile(ws[:, :ts, :ts], (1, bs, bs))
    bt_small = jnp.tile(jnp.transpose(b_s[l][:, :ts]), (bs, 1))
    yb, vn = _gmlp(gu, gv, gg, w_small, bt_small, ms, ts, True)
    prev = jnp.pad(state_conv[l], ((0, 0), (0, ts - (CONV_W - 1)), (0, 0))).reshape(ms, f)
    ys, a_s = _layer_dense_tail(xs, ya, yb, sa, sb, wts, ms, ms, prev, ts)

    return (yp.reshape(1, sp, d),
            ys.reshape(bs, ts, d),
            kf.reshape(1, 1, sp, N_HEADS, HEAD_DIM),
            vf.reshape(1, 1, sp, N_HEADS, HEAD_DIM),
            lf[:, :N_HEADS].reshape(1, 1, sp, N_HEADS),
            st_p[st_p.shape[0] - (CONV_W - 1):].reshape(1, 1, CONV_W - 1, f),
            kf_s.reshape(1, bs, ts, N_HEADS, HEAD_DIM),
            vf_s.reshape(1, bs, ts, N_HEADS, HEAD_DIM),
            lf_s[:, :N_HEADS].reshape(1, bs, ts, N_HEADS),
            vn.reshape(1, bs, ts, GM_W),
            a_s.reshape(bs, ts, f)[:, ts - (CONV_W - 1):].reshape(1, bs, CONV_W - 1, f))
```

```python
import functools

import jax
import jax.numpy as jnp
import numpy as np
from jax import lax
from jax.experimental import pallas as pl
from jax.experimental.pallas import tpu as pltpu

F32 = jnp.float32
BF16 = jnp.bfloat16

LANES = 128
SUBLANES = 8
VMEM_LIMIT = 56 * 1024 * 1024

HEAD_DIM = 128
N_HEADS = 16
ATT_W = N_HEADS * HEAD_DIM
GM_GROUPS = 16
GM_GROUP_W = 128
GM_W = GM_GROUPS * GM_GROUP_W
CHUNK = 128
CONV_W = 3
EPS = 1e-6
NEG = -1e30
SCALE = HEAD_DIM ** -0.5
LOG2E = float(np.log2(np.e))
Q_FOLD = SCALE * LOG2E

PROJ_TN = 512
MERGE_TN = 512
FFN_TN = 256
PAGES_PER_STEP = 8


def _cparams(n_axes):
    return pltpu.CompilerParams(dimension_semantics=("arbitrary",) * n_axes, vmem_limit_bytes=VMEM_LIMIT)


def _dot(a, b):
    return jnp.dot(a, b, preferred_element_type=F32)


def _dot_nt(a, b):
    return lax.dot_general(a, b, (((1,), (1,)), ((), ())), preferred_element_type=F32)


def _rms_rows(x, g):
    return x * lax.rsqrt(jnp.mean(x * x, axis=-1, keepdims=True) + EPS) * g


def _sigmoid(x):
    return 0.5 + 0.5 * jnp.tanh(0.5 * x)


def _gelu_tanh(x):
    return 0.5 * x * (1.0 + jnp.tanh(np.sqrt(2.0 / np.pi).astype(np.float32) * (x + 0.044715 * (x * x * x))))


def _log_sigmoid(x):
    return jnp.minimum(x, 0.0) - jnp.log(1.0 + jnp.exp(-jnp.abs(x)))


_SEC = {"q": (0, 4), "k": (4, 4), "v": (8, 4), "u": (12, 4), "vg": (16, 4), "ga": (20, 8), "gb": (28, 8)}


_N_QKV = 12
ROW_CHAINS = 1


def _inproj_body(x_ref, g1_ref, wq_ref, wr_ref, wf_ref, bf_ref, qg_ref, kg_ref,
                 q_ref, kf_ref, kb_ref, vf_ref, vb_ref, gu_ref, gv_ref, sa_ref, sb_ref, lf_ref, h_scr):
    j = pl.program_id(1)

    @pl.when(j == 0)
    def _():
        hb = _rms_rows(x_ref[...], g1_ref[...]).astype(BF16)
        h_scr[...] = hb
        lf_ref[...] = _log_sigmoid(_dot(hb, wf_ref[...]) + bf_ref[...])

    ch = h_scr.shape[0] // ROW_CHAINS

    def section(name, w_ref, epilogue):
        lo, n = _SEC[name]

        @pl.when(jnp.logical_and(j >= lo, j < lo + n))
        def _():
            w = w_ref[...]
            for c in range(ROW_CHAINS):
                rows = slice(c * ch, (c + 1) * ch)
                epilogue(rows, _dot(h_scr[rows, :], w))

    heads = [slice(h * HEAD_DIM, (h + 1) * HEAD_DIM) for h in range(PROJ_TN // HEAD_DIM)]

    def q_epi(rows, acc):
        for sl in heads:
            q_ref[rows, sl] = (_rms_rows(acc[:, sl], qg_ref[...]) * Q_FOLD).astype(BF16)

    def k_epi(rows, acc):
        for sl in heads:
            y = _rms_rows(acc[:, sl], kg_ref[...])
            kf_ref[rows, sl] = y
            kb_ref[rows, sl] = y.astype(BF16)

    def v_epi(rows, acc):
        vf_ref[rows, :] = acc
        vb_ref[rows, :] = acc.astype(BF16)

    def u_epi(rows, acc):
        gu_ref[rows, :] = _gelu_tanh(acc).astype(BF16)

    def vg_epi(rows, acc):
        gv_ref[rows, :] = _gelu_tanh(acc)

    def ga_epi(rows, acc):
        sa_ref[rows, :] = _sigmoid(acc).astype(BF16)

    def gb_epi(rows, acc):
        sb_ref[rows, :] = _sigmoid(acc).astype(BF16)

    section("q", wq_ref, q_epi)
    section("k", wq_ref, k_epi)
    section("v", wq_ref, v_epi)
    section("u", wr_ref, u_epi)
    section("vg", wr_ref, vg_epi)
    section("ga", wr_ref, ga_epi)
    section("gb", wr_ref, gb_epi)


def _inproj(x, g1, wq, wr, wf, bfp, qg, kg, tm):
    m, d = x.shape
    nq = wq.shape[1] // PROJ_TN
    nr = wr.shape[1] // PROJ_TN
    assert nq == _N_QKV

    def sec(name):
        lo, n = _SEC[name]
        return pl.BlockSpec((tm, PROJ_TN), lambda i, j: (i, jnp.clip(j - lo, 0, n - 1)))

    def shp(name, dt):
        return jax.ShapeDtypeStruct((m, _SEC[name][1] * PROJ_TN), dt)

    const = lambda i, j: (0, 0)
    return pl.pallas_call(
        _inproj_body,
        grid=(m // tm, nq + nr),
        in_specs=[pl.BlockSpec((tm, d), lambda i, j: (i, 0)),
                  pl.BlockSpec((1, d), const),
                  pl.BlockSpec((d, PROJ_TN), lambda i, j: (0, jnp.minimum(j, nq - 1))),
                  pl.BlockSpec((d, PROJ_TN), lambda i, j: (0, jnp.maximum(j - nq, 0))),
                  pl.BlockSpec((d, LANES), const),
                  pl.BlockSpec((1, LANES), const),
                  pl.BlockSpec((1, HEAD_DIM), const),
                  pl.BlockSpec((1, HEAD_DIM), const)],
        out_specs=[sec("q"), sec("k"), sec("k"), sec("v"), sec("v"), sec("u"), sec("vg"), sec("ga"), sec("gb"),
                   pl.BlockSpec((tm, LANES), lambda i, j: (i, 0))],
        out_shape=[shp("q", BF16), shp("k", F32), shp("k", BF16), shp("v", F32), shp("v", BF16),
                   shp("u", BF16), shp("vg", F32), shp("ga", BF16), shp("gb", BF16),
                   jax.ShapeDtypeStruct((m, LANES), F32)],
        scratch_shapes=[pltpu.VMEM((tm, d), BF16)],
        compiler_params=_cparams(2),
        name="inproj",
    )(x, g1, wq, wr, wf, bfp, qg, kg)


def _lane_cumsum(x, shifts, lane):
    for sh in shifts:
        x = x + jnp.where(lane >= sh, pltpu.roll(x, sh, axis=1), 0.0)
    return x


def _negcumsum_body(lf_ref, o_ref):
    nblk = lf_ref.shape[0] // LANES
    lane = lax.broadcasted_iota(jnp.int32, (N_HEADS, LANES), 1)

    def body(b, carry):
        off = pl.multiple_of(b * LANES, LANES)
        xt = jnp.transpose(lf_ref[pl.ds(off, LANES), :])[:N_HEADS, :]
        c = _lane_cumsum(xt, (1, 2, 4, 8, 16, 32, 64), lane) + carry
        o_ref[:, pl.ds(off, LANES)] = c * (-LOG2E)
        return c[:, LANES - 1:LANES]

    lax.fori_loop(0, nblk, body, jnp.zeros((N_HEADS, 1), F32))


def _negcumsum(lf):
    s = lf.shape[0]
    return pl.pallas_call(
        _negcumsum_body,
        out_shape=jax.ShapeDtypeStruct((N_HEADS, s), F32),
        compiler_params=pltpu.CompilerParams(vmem_limit_bytes=VMEM_LIMIT),
        name="negcumsum",
    )(lf)


def _fox_prompt_body(q_ref, k_ref, v_ref, nc_ref, o_ref, m_scr, l_scr, acc_scr, *, tq, rq):
    qi = pl.program_id(1)
    nrc = tq // rq
    m_scr[...] = jnp.full(m_scr.shape, NEG, F32)
    l_scr[...] = jnp.zeros(l_scr.shape, F32)
    acc_scr[...] = jnp.zeros(acc_scr.shape, F32)

    def update(rc, zs, vs):
        rows = slice(rc * rq, (rc + 1) * rq)
        m_old = m_scr[rows, :]
        m_new = m_old
        for z in zs:
            m_new = jnp.maximum(m_new, jnp.max(z, axis=-1, keepdims=True))
        alpha = jnp.exp2(m_old - m_new)
        l_new = alpha * l_scr[rows, :]
        acc = alpha * acc_scr[rows, :]
        for z, v in zip(zs, vs):
            pr = jnp.exp2(z - m_new)
            l_new = l_new + jnp.sum(pr, axis=-1, keepdims=True)
            acc = acc + _dot(pr.astype(BF16), v)
        l_scr[rows, :] = l_new
        acc_scr[rows, :] = acc
        m_scr[rows, :] = m_new

    def scores(rc, start, n):
        return (_dot_nt(q_ref[rc * rq:(rc + 1) * rq, :], k_ref[pl.ds(start, n), :]) + nc_ref[:, pl.ds(start, n)])

    def below_diagonal(ki, carry):
        off = pl.multiple_of(ki * tq, tq)
        v = v_ref[pl.ds(off, tq), :]
        for rc in range(nrc):
            update(rc, [scores(rc, off, tq)], [v])
        return carry

    lax.fori_loop(0, qi, below_diagonal, 0)

    off = pl.multiple_of(qi * tq, tq)
    causal = (lax.broadcasted_iota(jnp.int32, (rq, rq), 0) >= lax.broadcasted_iota(jnp.int32, (rq, rq), 1))
    for rc in range(nrc):
        n = rc * rq
        dstart = pl.multiple_of(off + n, rq)
        zs = [jnp.where(causal, scores(rc, dstart, rq), NEG)]
        vs = [v_ref[pl.ds(dstart, rq), :]]
        if n:
            zs.append(scores(rc, off, n))
            vs.append(v_ref[pl.ds(off, n), :])
        update(rc, zs, vs)
    o_ref[...] = (acc_scr[...] / l_scr[...]).astype(o_ref.dtype)


def _fox_prompt(qb, kb, vb, negc, tq=2048, rq=256):
    s = qb.shape[0]
    head = lambda h, i: (0, h)
    return pl.pallas_call(
        functools.partial(_fox_prompt_body, tq=tq, rq=rq),
        grid=(N_HEADS, s // tq),
        in_specs=[pl.BlockSpec((tq, HEAD_DIM), lambda h, i: (i, h)),
                  pl.BlockSpec((s, HEAD_DIM), head),
                  pl.BlockSpec((s, HEAD_DIM), head),
                  pl.BlockSpec((None, 1, s), lambda h, i: (h, 0, 0))],
        out_specs=pl.BlockSpec((tq, HEAD_DIM), lambda h, i: (i, h)),
        out_shape=jax.ShapeDtypeStruct((s, ATT_W), BF16),
        scratch_shapes=[pltpu.VMEM((tq, 1), F32), pltpu.VMEM((tq, 1), F32), pltpu.VMEM((tq, HEAD_DIM), F32)],
        compiler_params=_cparams(2),
        name="fox_prompt",
    )(qb, kb, vb, negc.reshape(N_HEADS, 1, s))


def _page_negcumsum(lf, run, lane, rowj):
    x = _lane_cumsum(lf, (16, 32, 64), lane)
    t = jnp.where(lane >= LANES - N_HEADS, x, 0.0)
    for sh in (16, 32, 64):
        t = t + pltpu.roll(t, sh, axis=1)
    e = t
    for sh in (1, 2, 4, 8):
        e = e + jnp.where(rowj >= sh, pltpu.roll(e, sh, axis=0), 0.0)
    c = x + (e - t) + run
    return -c, run + e[N_HEADS - 1:N_HEADS, :]


def _fox_sample_body(pt_ref, q_ref, *refs, npg):
    del pt_ref
    k_refs, v_refs, lf_refs = refs[:npg], refs[npg:2 * npg], refs[2 * npg:3 * npg]
    kn_ref, vn_ref, lfn_ref, o_ref, m_scr, l_scr, acc_scr, run_scr = refs[3 * npg:]
    g = pl.program_id(1)
    tiles = k_refs[0].shape[0] // LANES

    @pl.when(g == 0)
    def _():
        m_scr[...] = jnp.full(m_scr.shape, NEG, F32)
        l_scr[...] = jnp.zeros(l_scr.shape, F32)
        acc_scr[...] = jnp.zeros(acc_scr.shape, F32)
        run_scr[...] = jnp.zeros(run_scr.shape, F32)

    lane = lax.broadcasted_iota(jnp.int32, (SUBLANES, LANES), 1)
    sub = lax.broadcasted_iota(jnp.int32, (SUBLANES, LANES), 0)
    head_is = [(lane & (N_HEADS - 1)) == h for h in range(N_HEADS)]
    lane16 = lax.broadcasted_iota(jnp.int32, (N_HEADS, LANES), 1)
    row16 = lax.broadcasted_iota(jnp.int32, (N_HEADS, LANES), 0)
    q = q_ref[...]

    def compact(tile):
        zc = tile[(N_HEADS - 1) * SUBLANES:, :]
        for h in range(N_HEADS - 2, -1, -1):
            zc = jnp.where(head_is[h], tile[h * SUBLANES:(h + 1) * SUBLANES, :], zc)
        return zc

    def spread(pc):
        return jnp.concatenate([jnp.where(head_is[h], pc, 0.0) for h in range(N_HEADS)], axis=0)

    def over_tokens(x, op):
        for sh in (16, 32, 64):
            x = op(x, pltpu.roll(x, sh, axis=1))
        return x

    def per_row(x):
        return jnp.concatenate([jnp.broadcast_to(x[:, h:h + 1], (SUBLANES, LANES)) for h in range(N_HEADS)], axis=0)

    def attend(zcs, vbs, m, l, acc):
        zmax = zcs[0]
        for z in zcs[1:]:
            zmax = jnp.maximum(zmax, z)
        m_new = jnp.maximum(m, over_tokens(zmax, jnp.maximum))
        alpha = jnp.exp2(m - m_new)
        per = len(zcs) // len(vbs)
        psum = None
        pv = None
        for i, vb in enumerate(vbs):
            ps = [jnp.exp2(z - m_new) for z in zcs[i * per:(i + 1) * per]]
            for pp in ps:
                psum = pp if psum is None else psum + pp
            pm = jnp.concatenate([spread(pp) for pp in ps], axis=1) if per > 1 else spread(ps[0])
            d = _dot(pm.astype(BF16), vb)
            pv = d if pv is None else pv + d
        l = alpha * l + over_tokens(psum, jnp.add)
        return m_new, l, per_row(alpha) * acc + pv

    run = run_scr[...]
    zcs = []
    for p in range(npg):
        gt = _dot_nt(q, k_refs[p][...].astype(BF16))
        negc, run = _page_negcumsum(lf_refs[p][...], run, lane16, row16)
        negc = negc * LOG2E
        zcs += [compact(gt[:, t * LANES:(t + 1) * LANES]) + negc[t:t + 1, :] for t in range(tiles)]
    m, l, acc = attend(zcs, [v_refs[p][...].astype(BF16) for p in range(npg)],
                       m_scr[...], l_scr[...], acc_scr[...])
    m_scr[...], l_scr[...], acc_scr[...], run_scr[...] = m, l, acc, run

    @pl.when(g == pl.num_programs(1) - 1)
    def _():
        cn = _lane_cumsum(lfn_ref[...], (16, 32, 64), lane16[:1]) + run
        zn = compact(_dot_nt(q, kn_ref[...])) - cn * LOG2E
        _, l2, acc2 = attend([jnp.where((lane >> 4) <= sub, zn, NEG)], [vn_ref[...]], m, l, acc)
        o_ref[...] = acc2 / per_row(l2)


def _fox_sample(page_table, qcat, ck, cv, clf, kn, vn, lfn):
    nb = qcat.shape[0]
    n_pages = page_table.shape[1]
    npg = PAGES_PER_STEP
    rows = N_HEADS * SUBLANES
    prow = ck.shape[1]

    def page(p):
        return lambda b, g, pt: (pt[b * n_pages + g * npg + p], 0, 0)

    per_b = lambda b, g, pt: (b, 0, 0)
    in_specs = ([pl.BlockSpec((None, rows, HEAD_DIM), per_b)]
                + [pl.BlockSpec((None, prow, HEAD_DIM), page(p)) for p in range(npg)]
                + [pl.BlockSpec((None, prow, HEAD_DIM), page(p)) for p in range(npg)]
                + [pl.BlockSpec((None, N_HEADS, LANES), page(p)) for p in range(npg)]
                + [pl.BlockSpec((None, rows, HEAD_DIM), per_b),
                   pl.BlockSpec((None, rows, HEAD_DIM), per_b),
                   pl.BlockSpec((None, 1, LANES), per_b)])
    grid_spec = pltpu.PrefetchScalarGridSpec(
        num_scalar_prefetch=1,
        grid=(nb, n_pages // npg),
        in_specs=in_specs,
        out_specs=pl.BlockSpec((None, rows, HEAD_DIM), per_b),
        scratch_shapes=[pltpu.VMEM((SUBLANES, LANES), F32), pltpu.VMEM((SUBLANES, LANES), F32),
                        pltpu.VMEM((rows, HEAD_DIM), F32), pltpu.VMEM((1, LANES), F32)],
    )
    return pl.pallas_call(
        functools.partial(_fox_sample_body, npg=npg),
        grid_spec=grid_spec,
        out_shape=jax.ShapeDtypeStruct((nb, rows, HEAD_DIM), F32),
        compiler_params=_cparams(2),
        name="fox_sample",
    )(page_table.reshape(-1), qcat, *([ck] * npg), *([cv] * npg), *([clf] * npg), kn, vn, lfn)


def _gmlp_body(gu_ref, gv_ref, gg_ref, w_ref, bt_ref, yb_ref, *vn_ref, blk):
    r = gv_ref.shape[0]
    vn = _rms_rows(gv_ref[...], gg_ref[...])
    if vn_ref:
        vn_ref[0][...] = vn
    vnb = vn.astype(BF16)
    rr = lax.broadcasted_iota(jnp.int32, (r, r), 0)
    cc = lax.broadcasted_iota(jnp.int32, (r, r), 1)
    sh = blk.bit_length() - 1
    keep = jnp.logical_and((rr >> sh) == (cc >> sh), (cc & (blk - 1)) <= (rr & (blk - 1)))
    for g in range(GM_GROUPS):
        sl = slice(g * GM_GROUP_W, (g + 1) * GM_GROUP_W)
        w = jnp.where(keep, w_ref[g], jnp.zeros((), BF16))
        mixed = _dot(w, vnb[:, sl]) + bt_ref[:, g:g + 1]
        yb_ref[:, sl] = (gu_ref[:, sl].astype(F32) * mixed).astype(BF16)


def _gmlp(gu, gv, gg, w, bt, rows, blk, want_vn):
    m = gu.shape[0]
    row_spec = pl.BlockSpec((rows, GM_W), lambda i: (i, 0))
    out_shape = [jax.ShapeDtypeStruct((m, GM_W), BF16)]
    out_specs = [row_spec]
    if want_vn:
        out_shape.append(jax.ShapeDtypeStruct((m, GM_W), F32))
        out_specs.append(row_spec)
    return pl.pallas_call(
        functools.partial(_gmlp_body, blk=blk),
        grid=(m // rows,),
        in_specs=[row_spec, row_spec,
                  pl.BlockSpec((1, GM_W), lambda i: (0, 0)),
                  pl.BlockSpec((GM_GROUPS, rows, rows), lambda i: (0, 0, 0)),
                  pl.BlockSpec((rows, GM_GROUPS), lambda i: (0, 0))],
        out_specs=out_specs,
        out_shape=out_shape,
        compiler_params=_cparams(1),
        name="gmlp",
    )(gu, gv, gg, w, bt)


def _merge_body(ya_ref, yb_ref, woa_ref, wob_ref, sa_ref, sb_ref, m_ref):
    a = _dot(ya_ref[...], woa_ref[...])
    b = _dot(yb_ref[...], wob_ref[...])
    m_ref[...] = (sa_ref[...].astype(F32) * a + sb_ref[...].astype(F32) * b).astype(BF16)


def _merge(ya, yb, woa, wob, sa, sb, tm, tn=MERGE_TN):
    m, ka = ya.shape
    n = woa.shape[1]
    nt = n // tn
    wtile = pl.BlockSpec((ka, tn), lambda i, j: (0, j))
    return pl.pallas_call(
        _merge_body,
        grid=(m // tm, nt),
        in_specs=[pl.BlockSpec((tm, ka), lambda i, j: (i, 0)),
                  pl.BlockSpec((tm, ka), lambda i, j: (i, 0)),
                  wtile,
                  wtile,
                  pl.BlockSpec((tm, tn), lambda i, j: (i, j)),
                  pl.BlockSpec((tm, tn), lambda i, j: (i, j))],
        out_specs=pl.BlockSpec((tm, tn), lambda i, j: (i, j)),
        out_shape=jax.ShapeDtypeStruct((m, n), BF16),
        compiler_params=_cparams(2),
        name="merge",
    )(ya, yb, woa, wob, sa, sb)


def _resid_mm_body(a_ref, w_ref, x_ref, o_ref):
    o_ref[...] = x_ref[...] + _dot(a_ref[...], w_ref[...])


def _resid_mm(a, w, x, tm, tn, name):
    m, k = a.shape
    n = w.shape[1]
    return pl.pallas_call(
        _resid_mm_body,
        grid=(m // tm, n // tn),
        in_specs=[pl.BlockSpec((tm, k), lambda i, j: (i, 0)),
                  pl.BlockSpec((k, tn), lambda i, j: (0, j)),
                  pl.BlockSpec((tm, tn), lambda i, j: (i, j))],
        out_specs=pl.BlockSpec((tm, tn), lambda i, j: (i, j)),
        out_shape=jax.ShapeDtypeStruct((m, n), F32),
        compiler_params=_cparams(2),
        name=name,
    )(a, w, x)


def _ffn_up_body(x_ref, g2_ref, wu_ref, wg_ref, cw_ref, cb_ref, *refs, seq_rows):
    i = pl.program_id(0)
    j = pl.program_id(1)
    if seq_rows is None:
        act_ref, st_ref, h_scr, carry_scr = refs
    else:
        prev_ref, act_ref, st_ref, h_scr = refs

    @pl.when(j == 0)
    def _():
        h_scr[...] = _rms_rows(x_ref[...], g2_ref[...]).astype(BF16)

    ch = h_scr.shape[0] // ROW_CHAINS
    wu = wu_ref[...]
    wg = wg_ref[...]
    row = lax.broadcasted_iota(jnp.int32, (ch, wu.shape[1]), 0)
    if seq_rows is None:
        @pl.when(i == 0)
        def _():
            carry_scr[j] = jnp.zeros(carry_scr.shape[1:], F32)

        above = carry_scr[j]
    for ci in range(ROW_CHAINS):
        rows = slice(ci * ch, (ci + 1) * ch)
        h = h_scr[rows, :]
        a = _dot(h, wu)
        gate = _dot(h, wg)
        if seq_rows is None:
            p1 = above[SUBLANES - 1:SUBLANES, :]
            p2 = above[SUBLANES - 2:SUBLANES - 1, :]
            a1 = jnp.where(row == 0, p1, pltpu.roll(a, 1, axis=0))
            a2 = jnp.where(row == 0, p2, jnp.where(row == 1, p1, pltpu.roll(a, 2, axis=0)))
            above = a[ch - SUBLANES:, :]
        else:
            t = row & (seq_rows - 1)
            prev = prev_ref[rows, :]
            a1 = jnp.where(t == 0, pltpu.roll(prev, ch - 1, axis=0), pltpu.roll(a, 1, axis=0))
            a2 = jnp.where(t < 2, prev, pltpu.roll(a, 2, axis=0))
            st_ref[rows, :] = a
        c = cb_ref[...] + cw_ref[0:1, :] * a2 + cw_ref[1:2, :] * a1 + cw_ref[2:3, :] * a
        hc = 0.5 * c
        act_ref[rows, :] = ((hc + hc * jnp.tanh(hc)) * gate).astype(BF16)
    if seq_rows is None:
        carry_scr[j] = above
        st_ref[...] = above


def _ffn_up(x, g2, wu, wg, cw, cb, tm, prev=None, seq_rows=None, tn=FFN_TN):
    m, d = x.shape
    f = wu.shape[1]
    nj = f // tn
    col = lambda i, j: (0, j)
    wtile = pl.BlockSpec((d, tn), col)
    x_mode = {"pipeline_mode": pl.Buffered(1)} if tm * d * 4 > VMEM_LIMIT // 4 else {}
    in_specs = [pl.BlockSpec((tm, d), lambda i, j: (i, 0), **x_mode),
                pl.BlockSpec((1, d), lambda i, j: (0, 0)),
                wtile,
                wtile,
                pl.BlockSpec((CONV_W, tn), col),
                pl.BlockSpec((1, tn), col)]
    args = [x, g2, wu, wg, cw, cb]
    scratch = [pltpu.VMEM((tm, d), BF16)]
    if seq_rows is None:
        st_shape = jax.ShapeDtypeStruct((m // tm * SUBLANES, f), F32)
        st_spec = pl.BlockSpec((SUBLANES, tn), lambda i, j: (i, j))
        scratch.append(pltpu.VMEM((nj, SUBLANES, tn), F32))
    else:
        in_specs.append(pl.BlockSpec((tm, tn), lambda i, j: (i, j)))
        args.append(prev)
        st_shape = jax.ShapeDtypeStruct((m, f), F32)
        st_spec = pl.BlockSpec((tm, tn), lambda i, j: (i, j))
    return pl.pallas_call(
        functools.partial(_ffn_up_body, seq_rows=seq_rows),
        grid=(m // tm, nj),
        in_specs=in_specs,
        out_specs=[pl.BlockSpec((tm, tn), lambda i, j: (i, j)), st_spec],
        out_shape=[jax.ShapeDtypeStruct((m, f), BF16), st_shape],
        scratch_shapes=scratch,
        compiler_params=_cparams(2),
        name="ffn_up",
    )(*args)


def _layer_dense_tail(x, ya, yb, sa, sb, wts, tm, tm_down, prev=None, seq_rows=None):
    mrg = _merge(ya, yb, wts["oa"], wts["ob"], sa, sb, tm)
    x1 = _resid_mm(mrg, wts["out"], x, tm, MERGE_TN, "out_proj")
    act, st = _ffn_up(x1, wts["g2"], wts["up"], wts["gate"], wts["cw"], wts["cb"], tm, prev, seq_rows)
    y = _resid_mm(act, wts["down"], x1, tm_down, FFN_TN, "ffn_down")
    return y, st


def kernel(x_prompt, x_sample, cache_k, cache_v, cache_logf, state_conv, page_table, norm1_g, w_in, b_f,
           q_norm_g, k_norm_g, gm_norm_g, w_s, b_s, w_oa, w_ob, w_out, norm2_g, w_up, w_gate, conv_w, conv_b,
           w_down):
    depth = w_in.shape[0]
    assert depth == 1
    bp, sp, d = x_prompt.shape
    assert bp == 1
    bs, ts, _ = x_sample.shape
    assert ts == SUBLANES
    f = w_up.shape[-1]
    l = 0

    qkv_w = 3 * ATT_W
    wi = w_in[l]
    wq = wi[:, :qkv_w].astype(BF16)
    wr = wi[:, qkv_w + N_HEADS:].astype(BF16)
    wf = jnp.pad(wi[:, qkv_w:qkv_w + N_HEADS], ((0, 0), (0, LANES - N_HEADS))).astype(BF16)
    bfp = jnp.pad(b_f[l], (0, LANES - N_HEADS)).reshape(1, LANES)
    g1 = norm1_g[l].reshape(1, d)
    qg = q_norm_g[l].reshape(1, HEAD_DIM)
    kg = k_norm_g[l].reshape(1, HEAD_DIM)
    gg = gm_norm_g[l].reshape(1, GM_W)
    wts = {"oa": w_oa[l].astype(BF16), "ob": w_ob[l].astype(BF16), "out": w_out[l].astype(BF16),
           "g2": norm2_g[l].reshape(1, d), "up": w_up[l].astype(BF16), "gate": w_gate[l].astype(BF16),
           "cw": conv_w[l], "cb": conv_b[l].reshape(1, f), "down": w_down[l].astype(BF16)}
    ws = w_s[l].astype(BF16)

    xp = x_prompt.reshape(sp, d)
    q, kf, kb, vf, vb, gu, gv, sa, sb, lf = _inproj(xp, g1, wq, wr, wf, bfp, qg, kg, tm=512)
    ya = _fox_prompt(q, kb, vb, _negcumsum(lf))
    (yb,) = _gmlp(gu, gv, gg, ws, jnp.transpose(b_s[l]), CHUNK, CHUNK, False)
    yp, st_p = _layer_dense_tail(xp, ya, yb, sa, sb, wts, 1024, 512)

    ms = bs * ts
    xs = x_sample.reshape(ms, d)
    q, kf_s, kb_s, vf_s, vb_s, gu, gv, sa, sb, lf_s = _inproj(xs, g1, wq, wr, wf, bfp, qg, kg, tm=ms)
    rows = N_HEADS * ts
    qcat = jnp.transpose(q.reshape(bs, ts, N_HEADS, HEAD_DIM), (0, 2, 1, 3)).reshape(bs, rows, HEAD_DIM)
    n_pool = cache_k.shape[1]
    page_rows = cache_k.shape[2] * N_HEADS
    ck = cache_k[l].reshape(n_pool, page_rows, HEAD_DIM)
    cv = cache_v[l].reshape(n_pool, page_rows, HEAD_DIM)
    clf = cache_logf[l].reshape(n_pool, page_rows // LANES, LANES)
    o = _fox_sample(page_table, qcat, ck, cv, clf,
                    kb_s.reshape(bs, rows, HEAD_DIM), vb_s.reshape(bs, rows, HEAD_DIM),
                    lf_s[:, :N_HEADS].reshape(bs, 1, rows))
    ya = jnp.transpose(o.reshape(bs, N_HEADS, ts, HEAD_DIM), (0, 2, 1, 3)).reshape(ms, ATT_W).astype(BF16)
    w_small = jnp.tile(ws[:, :ts, :ts], (1, bs, bs))
    bt_small = jnp.tile(jnp.transpose(b_s[l][:, :ts]), (bs, 1))
    yb, vn = _gmlp(gu, gv, gg, w_small, bt_small, ms, ts, True)
    prev = jnp.pad(state_conv[l], ((0, 0), (0, ts - (CONV_W - 1)), (0, 0))).reshape(ms, f)
    ys, a_s = _layer_dense_tail(xs, ya, yb, sa, sb, wts, ms, ms, prev, ts)

    return (yp.reshape(1, sp, d),
            ys.reshape(bs, ts, d),
            kf.reshape(1, 1, sp, N_HEADS, HEAD_DIM),
            vf.reshape(1, 1, sp, N_HEADS, HEAD_DIM),
            lf[:, :N_HEADS].reshape(1, 1, sp, N_HEADS),
            st_p[st_p.shape[0] - (CONV_W - 1):].reshape(1, 1, CONV_W - 1, f),
            kf_s.reshape(1, bs, ts, N_HEADS, HEAD_DIM),
            vf_s.reshape(1, bs, ts, N_HEADS, HEAD_DIM),
            lf_s[:, :N_HEADS].reshape(1, bs, ts, N_HEADS),
            vn.reshape(1, bs, ts, GM_W),
            a_s.reshape(bs, ts, f)[:, ts - (CONV_W - 1):].reshape(1, bs, CONV_W - 1, f))
```

```python
import functools

import jax
import jax.numpy as jnp
import numpy as np
from jax import lax
from jax.experimental import pallas as pl
from jax.experimental.pallas import tpu as pltpu

F32 = jnp.float32
BF16 = jnp.bfloat16

LANES = 128
SUBLANES = 8
VMEM_LIMIT = 56 * 1024 * 1024

HEAD_DIM = 128
N_HEADS = 16
ATT_W = N_HEADS * HEAD_DIM
GM_GROUPS = 16
GM_GROUP_W = 128
GM_W = GM_GROUPS * GM_GROUP_W
CHUNK = 128
CONV_W = 3
EPS = 1e-6
NEG = -1e30
SCALE = HEAD_DIM ** -0.5
LOG2E = float(np.log2(np.e))
Q_FOLD = SCALE * LOG2E

PROJ_TN = 512
MERGE_TN = 512
FFN_TN = 256
DOWN_TN = 512
GMLP_CHUNKS_PER_STEP = 4
PAGES_PER_STEP = 8


def _cparams(n_axes):
    return pltpu.CompilerParams(dimension_semantics=("arbitrary",) * n_axes, vmem_limit_bytes=VMEM_LIMIT)


def _dot(a, b):
    return jnp.dot(a, b, preferred_element_type=F32)


def _dot_nt(a, b):
    return lax.dot_general(a, b, (((1,), (1,)), ((), ())), preferred_element_type=F32)


def _rms_rows(x, g):
    return x * lax.rsqrt(jnp.mean(x * x, axis=-1, keepdims=True) + EPS) * g


def _sigmoid(x):
    return 0.5 + 0.5 * jnp.tanh(0.5 * x)


def _gelu_tanh(x):
    return 0.5 * x * (1.0 + jnp.tanh(np.sqrt(2.0 / np.pi).astype(np.float32) * (x + 0.044715 * (x * x * x))))


def _log_sigmoid(x):
    return jnp.minimum(x, 0.0) - jnp.log(1.0 + jnp.exp(-jnp.abs(x)))


ROW_CHAINS = 1


def _rmsnorm_body(x_ref, g_ref, o_ref):
    o_ref[...] = _rms_rows(x_ref[...], g_ref[...]).astype(BF16)


def _rmsnorm(x, g, tm):
    m, d = x.shape
    return pl.pallas_call(
        _rmsnorm_body,
        grid=(m // tm,),
        in_specs=[pl.BlockSpec((tm, d), lambda i: (i, 0)), pl.BlockSpec((1, d), lambda i: (0, 0))],
        out_specs=pl.BlockSpec((tm, d), lambda i: (i, 0)),
        out_shape=jax.ShapeDtypeStruct((m, d), BF16),
        compiler_params=_cparams(1),
        name="rmsnorm",
    )(x, g)


def _sections(h_ref, w_ref, table):
    j = pl.program_id(1)
    for lo, n, epilogue in table:
        @pl.when(jnp.logical_and(j >= lo, j < lo + n))
        def _(epilogue=epilogue):
            epilogue(_dot(h_ref[...], w_ref[...]))


_HEADS = [slice(h * HEAD_DIM, (h + 1) * HEAD_DIM) for h in range(PROJ_TN // HEAD_DIM)]
_SEC_QKV = {"q": (0, 4), "k": (4, 4), "v": (8, 4)}
_SEC_REST = {"u": (0, 4), "vg": (4, 4), "ga": (8, 8), "gb": (16, 8)}


def _proj_qkv_body(h_ref, w_ref, wf_ref, bf_ref, qg_ref, kg_ref, q_ref, kf_ref, kb_ref, vf_ref, vb_ref, lf_ref):
    @pl.when(pl.program_id(1) == 0)
    def _():
        lf_ref[...] = _log_sigmoid(_dot(h_ref[...], wf_ref[...]) + bf_ref[...])

    def q_epi(acc):
        for sl in _HEADS:
            q_ref[:, sl] = (_rms_rows(acc[:, sl], qg_ref[...]) * Q_FOLD).astype(BF16)

    def k_epi(acc):
        for sl in _HEADS:
            y = _rms_rows(acc[:, sl], kg_ref[...])
            kf_ref[:, sl] = y
            kb_ref[:, sl] = y.astype(BF16)

    def v_epi(acc):
        vf_ref[...] = acc
        vb_ref[...] = acc.astype(BF16)

    _sections(h_ref, w_ref, [(*_SEC_QKV["q"], q_epi), (*_SEC_QKV["k"], k_epi), (*_SEC_QKV["v"], v_epi)])


def _proj_rest_body(h_ref, w_ref, gu_ref, gv_ref, sa_ref, sb_ref):
    def u_epi(acc):
        gu_ref[...] = _gelu_tanh(acc).astype(BF16)

    def vg_epi(acc):
        gv_ref[...] = _gelu_tanh(acc)

    def ga_epi(acc):
        sa_ref[...] = _sigmoid(acc).astype(BF16)

    def gb_epi(acc):
        sb_ref[...] = _sigmoid(acc).astype(BF16)

    _sections(h_ref, w_ref, [(*_SEC_REST["u"], u_epi), (*_SEC_REST["vg"], vg_epi),
                             (*_SEC_REST["ga"], ga_epi), (*_SEC_REST["gb"], gb_epi)])


def _proj_call(body, name, h, w, extra, extra_specs, secs, outs, tm, tail_out=None):
    m, d = h.shape
    nj = w.shape[1] // PROJ_TN

    def sec(sname):
        lo, n = secs[sname]
        return pl.BlockSpec((tm, PROJ_TN), lambda i, j: (i, jnp.clip(j - lo, 0, n - 1)))

    out_specs = [sec(sname) for sname, _ in outs]
    out_shape = [jax.ShapeDtypeStruct((m, secs[sname][1] * PROJ_TN), dt) for sname, dt in outs]
    if tail_out:
        out_specs.append(pl.BlockSpec((tm, LANES), lambda i, j: (i, 0)))
        out_shape.append(jax.ShapeDtypeStruct((m, LANES), F32))
    return pl.pallas_call(
        body,
        grid=(m // tm, nj),
        in_specs=[pl.BlockSpec((tm, d), lambda i, j: (i, 0)),
                  pl.BlockSpec((d, PROJ_TN), lambda i, j: (0, j))] + extra_specs,
        out_specs=out_specs,
        out_shape=out_shape,
        compiler_params=_cparams(2),
        name=name,
    )(h, w, *extra)


def _inproj(x, g1, wq, wr, wf, bfp, qg, kg, tm_norm, tm):
    d = x.shape[1]
    h = _rmsnorm(x, g1, tm_norm)
    const = lambda i, j: (0, 0)
    q, kf, kb, vf, vb, lf = _proj_call(
        _proj_qkv_body, "proj_qkv", h, wq, [wf, bfp, qg, kg],
        [pl.BlockSpec((d, LANES), const), pl.BlockSpec((1, LANES), const),
         pl.BlockSpec((1, HEAD_DIM), const), pl.BlockSpec((1, HEAD_DIM), const)],
        _SEC_QKV, [("q", BF16), ("k", F32), ("k", BF16), ("v", F32), ("v", BF16)], tm, tail_out=True)
    gu, gv, sa, sb = _proj_call(
        _proj_rest_body, "proj_rest", h, wr, [], [],
        _SEC_REST, [("u", BF16), ("vg", F32), ("ga", BF16), ("gb", BF16)], tm)
    return q, kf, kb, vf, vb, gu, gv, sa, sb, lf


def _lane_cumsum(x, shifts, lane):
    for sh in shifts:
        x = x + jnp.where(lane >= sh, pltpu.roll(x, sh, axis=1), 0.0)
    return x


def _negcumsum_body(lf_ref, o_ref):
    nblk = lf_ref.shape[0] // LANES
    lane = lax.broadcasted_iota(jnp.int32, (N_HEADS, LANES), 1)

    def body(b, carry):
        off = pl.multiple_of(b * LANES, LANES)
        xt = jnp.transpose(lf_ref[pl.ds(off, LANES), :])[:N_HEADS, :]
        c = _lane_cumsum(xt, (1, 2, 4, 8, 16, 32, 64), lane) + carry
        o_ref[:, pl.ds(off, LANES)] = c * (-LOG2E)
        return c[:, LANES - 1:LANES]

    lax.fori_loop(0, nblk, body, jnp.zeros((N_HEADS, 1), F32))


def _negcumsum(lf):
    s = lf.shape[0]
    return pl.pallas_call(
        _negcumsum_body,
        out_shape=jax.ShapeDtypeStruct((N_HEADS, s), F32),
        compiler_params=pltpu.CompilerParams(vmem_limit_bytes=VMEM_LIMIT),
        name="negcumsum",
    )(lf)


def _fox_prompt_body(q_ref, k_ref, v_ref, nc_ref, o_ref, m_scr, l_scr, acc_scr, *, tq, rq):
    qi = pl.program_id(1)
    nrc = tq // rq
    m_scr[...] = jnp.full(m_scr.shape, NEG, F32)
    l_scr[...] = jnp.zeros(l_scr.shape, F32)
    acc_scr[...] = jnp.zeros(acc_scr.shape, F32)

    def update(rc, zs, vs):
        rows = slice(rc * rq, (rc + 1) * rq)
        m_old = m_scr[rows, :]
        m_new = m_old
        for z in zs:
            m_new = jnp.maximum(m_new, jnp.max(z, axis=-1, keepdims=True))
        alpha = jnp.exp2(m_old - m_new)
        l_new = alpha * l_scr[rows, :]
        acc = alpha * acc_scr[rows, :]
        for z, v in zip(zs, vs):
            pr = jnp.exp2(z - m_new)
            l_new = l_new + jnp.sum(pr, axis=-1, keepdims=True)
            acc = acc + _dot(pr.astype(BF16), v)
        l_scr[rows, :] = l_new
        acc_scr[rows, :] = acc
        m_scr[rows, :] = m_new

    def scores(rc, start, n):
        return (_dot_nt(q_ref[rc * rq:(rc + 1) * rq, :], k_ref[pl.ds(start, n), :]) + nc_ref[:, pl.ds(start, n)])

    def below_diagonal(ki, carry):
        off = pl.multiple_of(ki * tq, tq)
        v = v_ref[pl.ds(off, tq), :]
        for rc in range(nrc):
            update(rc, [scores(rc, off, tq)], [v])
        return carry

    lax.fori_loop(0, qi, below_diagonal, 0)

    off = pl.multiple_of(qi * tq, tq)
    causal = (lax.broadcasted_iota(jnp.int32, (rq, rq), 0) >= lax.broadcasted_iota(jnp.int32, (rq, rq), 1))
    for rc in range(nrc):
        n = rc * rq
        dstart = pl.multiple_of(off + n, rq)
        zs = [jnp.where(causal, scores(rc, dstart, rq), NEG)]
        vs = [v_ref[pl.ds(dstart, rq), :]]
        if n:
            zs.append(scores(rc, off, n))
            vs.append(v_ref[pl.ds(off, n), :])
        update(rc, zs, vs)
    o_ref[...] = (acc_scr[...] / l_scr[...]).astype(o_ref.dtype)


def _fox_prompt(qb, kb, vb, negc, tq=2048, rq=256):
    s = qb.shape[0]
    head = lambda h, i: (0, h)
    return pl.pallas_call(
        functools.partial(_fox_prompt_body, tq=tq, rq=rq),
        grid=(N_HEADS, s // tq),
        in_specs=[pl.BlockSpec((tq, HEAD_DIM), lambda h, i: (i, h)),
                  pl.BlockSpec((s, HEAD_DIM), head),
                  pl.BlockSpec((s, HEAD_DIM), head),
                  pl.BlockSpec((None, 1, s), lambda h, i: (h, 0, 0))],
        out_specs=pl.BlockSpec((tq, HEAD_DIM), lambda h, i: (i, h)),
        out_shape=jax.ShapeDtypeStruct((s, ATT_W), BF16),
        scratch_shapes=[pltpu.VMEM((tq, 1), F32), pltpu.VMEM((tq, 1), F32), pltpu.VMEM((tq, HEAD_DIM), F32)],
        compiler_params=_cparams(2),
        name="fox_prompt",
    )(qb, kb, vb, negc.reshape(N_HEADS, 1, s))


def _page_negcumsum(lf, run, lane, rowj):
    x = _lane_cumsum(lf, (16, 32, 64), lane)
    t = jnp.where(lane >= LANES - N_HEADS, x, 0.0)
    for sh in (16, 32, 64):
        t = t + pltpu.roll(t, sh, axis=1)
    e = t
    for sh in (1, 2, 4, 8):
        e = e + jnp.where(rowj >= sh, pltpu.roll(e, sh, axis=0), 0.0)
    c = x + (e - t) + run
    return -c, run + e[N_HEADS - 1:N_HEADS, :]


def _fox_sample_body(pt_ref, q_ref, *refs, npg):
    del pt_ref
    k_refs, v_refs, lf_refs = refs[:npg], refs[npg:2 * npg], refs[2 * npg:3 * npg]
    kn_ref, vn_ref, lfn_ref, o_ref, m_scr, l_scr, acc_scr, run_scr = refs[3 * npg:]
    g = pl.program_id(1)
    tiles = k_refs[0].shape[0] // LANES

    @pl.when(g == 0)
    def _():
        m_scr[...] = jnp.full(m_scr.shape, NEG, F32)
        l_scr[...] = jnp.zeros(l_scr.shape, F32)
        acc_scr[...] = jnp.zeros(acc_scr.shape, F32)
        run_scr[...] = jnp.zeros(run_scr.shape, F32)

    lane = lax.broadcasted_iota(jnp.int32, (SUBLANES, LANES), 1)
    sub = lax.broadcasted_iota(jnp.int32, (SUBLANES, LANES), 0)
    head_is = [(lane & (N_HEADS - 1)) == h for h in range(N_HEADS)]
    lane16 = lax.broadcasted_iota(jnp.int32, (N_HEADS, LANES), 1)
    row16 = lax.broadcasted_iota(jnp.int32, (N_HEADS, LANES), 0)
    q = q_ref[...]

    def compact(tile):
        zc = tile[(N_HEADS - 1) * SUBLANES:, :]
        for h in range(N_HEADS - 2, -1, -1):
            zc = jnp.where(head_is[h], tile[h * SUBLANES:(h + 1) * SUBLANES, :], zc)
        return zc

    def spread(pc):
        return jnp.concatenate([jnp.where(head_is[h], pc, 0.0) for h in range(N_HEADS)], axis=0)

    def over_tokens(x, op):
        for sh in (16, 32, 64):
            x = op(x, pltpu.roll(x, sh, axis=1))
        return x

    def per_row(x):
        return jnp.concatenate([jnp.broadcast_to(x[:, h:h + 1], (SUBLANES, LANES)) for h in range(N_HEADS)], axis=0)

    def attend(zcs, vbs, m, l, acc):
        zmax = zcs[0]
        for z in zcs[1:]:
            zmax = jnp.maximum(zmax, z)
        m_new = jnp.maximum(m, over_tokens(zmax, jnp.maximum))
        alpha = jnp.exp2(m - m_new)
        per = len(zcs) // len(vbs)
        psum = None
        pv = None
        for i, vb in enumerate(vbs):
            ps = [jnp.exp2(z - m_new) for z in zcs[i * per:(i + 1) * per]]
            for pp in ps:
                psum = pp if psum is None else psum + pp
            pm = jnp.concatenate([spread(pp) for pp in ps], axis=1) if per > 1 else spread(ps[0])
            d = _dot(pm.astype(BF16), vb)
            pv = d if pv is None else pv + d
        l = alpha * l + over_tokens(psum, jnp.add)
        return m_new, l, per_row(alpha) * acc + pv

    run = run_scr[...]
    zcs = []
    for p in range(npg):
        gt = _dot_nt(q, k_refs[p][...].astype(BF16))
        negc, run = _page_negcumsum(lf_refs[p][...], run, lane16, row16)
        negc = negc * LOG2E
        zcs += [compact(gt[:, t * LANES:(t + 1) * LANES]) + negc[t:t + 1, :] for t in range(tiles)]
    m, l, acc = attend(zcs, [v_refs[p][...].astype(BF16) for p in range(npg)],
                       m_scr[...], l_scr[...], acc_scr[...])
    m_scr[...], l_scr[...], acc_scr[...], run_scr[...] = m, l, acc, run

    @pl.when(g == pl.num_programs(1) - 1)
    def _():
        cn = _lane_cumsum(lfn_ref[...], (16, 32, 64), lane16[:1]) + run
        zn = compact(_dot_nt(q, kn_ref[...])) - cn * LOG2E
        _, l2, acc2 = attend([jnp.where((lane >> 4) <= sub, zn, NEG)], [vn_ref[...]], m, l, acc)
        o_ref[...] = acc2 / per_row(l2)


def _fox_sample(page_table, qcat, ck, cv, clf, kn, vn, lfn):
    nb = qcat.shape[0]
    n_pages = page_table.shape[1]
    npg = PAGES_PER_STEP
    rows = N_HEADS * SUBLANES
    prow = ck.shape[1]

    def page(p):
        return lambda b, g, pt: (pt[b * n_pages + g * npg + p], 0, 0)

    per_b = lambda b, g, pt: (b, 0, 0)
    in_specs = ([pl.BlockSpec((None, rows, HEAD_DIM), per_b)]
                + [pl.BlockSpec((None, prow, HEAD_DIM), page(p)) for p in range(npg)]
                + [pl.BlockSpec((None, prow, HEAD_DIM), page(p)) for p in range(npg)]
                + [pl.BlockSpec((None, N_HEADS, LANES), page(p)) for p in range(npg)]
                + [pl.BlockSpec((None, rows, HEAD_DIM), per_b),
                   pl.BlockSpec((None, rows, HEAD_DIM), per_b),
                   pl.BlockSpec((None, 1, LANES), per_b)])
    grid_spec = pltpu.PrefetchScalarGridSpec(
        num_scalar_prefetch=1,
        grid=(nb, n_pages // npg),
        in_specs=in_specs,
        out_specs=pl.BlockSpec((None, rows, HEAD_DIM), per_b),
        scratch_shapes=[pltpu.VMEM((SUBLANES, LANES), F32), pltpu.VMEM((SUBLANES, LANES), F32),
                        pltpu.VMEM((rows, HEAD_DIM), F32), pltpu.VMEM((1, LANES), F32)],
    )
    return pl.pallas_call(
        functools.partial(_fox_sample_body, npg=npg),
        grid_spec=grid_spec,
        out_shape=jax.ShapeDtypeStruct((nb, rows, HEAD_DIM), F32),
        compiler_params=_cparams(2),
        name="fox_sample",
    )(page_table.reshape(-1), qcat, *([ck] * npg), *([cv] * npg), *([clf] * npg), kn, vn, lfn)


def _gmlp_body(gu_ref, gv_ref, gg_ref, w_ref, bt_ref, yb_ref, *vn_ref, blk):
    r = gv_ref.shape[0]
    vn = _rms_rows(gv_ref[...], gg_ref[...])
    if vn_ref:
        vn_ref[0][...] = vn
    vnb = vn.astype(BF16)
    rr = lax.broadcasted_iota(jnp.int32, (r, r), 0)
    cc = lax.broadcasted_iota(jnp.int32, (r, r), 1)
    sh = blk.bit_length() - 1
    keep = jnp.logical_and((rr >> sh) == (cc >> sh), (cc & (blk - 1)) <= (rr & (blk - 1)))
    for g in range(GM_GROUPS):
        sl = slice(g * GM_GROUP_W, (g + 1) * GM_GROUP_W)
        w = jnp.where(keep, w_ref[g], jnp.zeros((), BF16))
        mixed = _dot(w, vnb[:, sl]) + bt_ref[:, g:g + 1]
        yb_ref[:, sl] = (gu_ref[:, sl].astype(F32) * mixed).astype(BF16)


def _gmlp(gu, gv, gg, w, bt, rows, blk, want_vn):
    m = gu.shape[0]
    row_spec = pl.BlockSpec((rows, GM_W), lambda i: (i, 0))
    out_shape = [jax.ShapeDtypeStruct((m, GM_W), BF16)]
    out_specs = [row_spec]
    if want_vn:
        out_shape.append(jax.ShapeDtypeStruct((m, GM_W), F32))
        out_specs.append(row_spec)
    return pl.pallas_call(
        functools.partial(_gmlp_body, blk=blk),
        grid=(m // rows,),
        in_specs=[row_spec, row_spec,
                  pl.BlockSpec((1, GM_W), lambda i: (0, 0)),
                  pl.BlockSpec((GM_GROUPS, rows, rows), lambda i: (0, 0, 0)),
                  pl.BlockSpec((rows, GM_GROUPS), lambda i: (0, 0))],
        out_specs=out_specs,
        out_shape=out_shape,
        compiler_params=_cparams(1),
        name="gmlp",
    )(gu, gv, gg, w, bt)


def _merge_body(ya_ref, yb_ref, woa_ref, wob_ref, sa_ref, sb_ref, m_ref):
    a = _dot(ya_ref[...], woa_ref[...])
    b = _dot(yb_ref[...], wob_ref[...])
    m_ref[...] = (sa_ref[...].astype(F32) * a + sb_ref[...].astype(F32) * b).astype(BF16)


def _merge(ya, yb, woa, wob, sa, sb, tm, tn=MERGE_TN):
    m, ka = ya.shape
    n = woa.shape[1]
    nt = n // tn
    wtile = pl.BlockSpec((ka, tn), lambda i, j: (0, j))
    return pl.pallas_call(
        _merge_body,
        grid=(m // tm, nt),
        in_specs=[pl.BlockSpec((tm, ka), lambda i, j: (i, 0)),
                  pl.BlockSpec((tm, ka), lambda i, j: (i, 0)),
                  wtile,
                  wtile,
                  pl.BlockSpec((tm, tn), lambda i, j: (i, j)),
                  pl.BlockSpec((tm, tn), lambda i, j: (i, j))],
        out_specs=pl.BlockSpec((tm, tn), lambda i, j: (i, j)),
        out_shape=jax.ShapeDtypeStruct((m, n), BF16),
        compiler_params=_cparams(2),
        name="merge",
    )(ya, yb, woa, wob, sa, sb)


def _resid_mm_body(a_ref, w_ref, x_ref, o_ref):
    o_ref[...] = x_ref[...] + _dot(a_ref[...], w_ref[...])


def _resid_mm(a, w, x, tm, tn, name):
    m, k = a.shape
    n = w.shape[1]
    return pl.pallas_call(
        _resid_mm_body,
        grid=(m // tm, n // tn),
        in_specs=[pl.BlockSpec((tm, k), lambda i, j: (i, 0)),
                  pl.BlockSpec((k, tn), lambda i, j: (0, j)),
                  pl.BlockSpec((tm, tn), lambda i, j: (i, j))],
        out_specs=pl.BlockSpec((tm, tn), lambda i, j: (i, j)),
        out_shape=jax.ShapeDtypeStruct((m, n), F32),
        compiler_params=_cparams(2),
        name=name,
    )(a, w, x)


def _ffn_up_body(x_ref, g2_ref, wu_ref, wg_ref, cw_ref, cb_ref, *refs, seq_rows):
    i = pl.program_id(0)
    j = pl.program_id(1)
    if seq_rows is None:
        act_ref, st_ref, h_scr, carry_scr = refs
    else:
        prev_ref, act_ref, st_ref, h_scr = refs

    @pl.when(j == 0)
    def _():
        h_scr[...] = _rms_rows(x_ref[...], g2_ref[...]).astype(BF16)

    ch = h_scr.shape[0] // ROW_CHAINS
    wu = wu_ref[...]
    wg = wg_ref[...]
    tn = wu.shape[1]

    def gated(c, g):
        hc = 0.5 * c
        return ((hc + hc * jnp.tanh(hc)) * g).astype(BF16)

    def conv(x2, x1, x0):
        return cb_ref[...] + cw_ref[0:1, :] * x2 + cw_ref[1:2, :] * x1 + cw_ref[2:3, :] * x0

    if seq_rows is None:
        @pl.when(i == 0)
        def _():
            carry_scr[j] = jnp.zeros(carry_scr.shape[1:], F32)

        above = carry_scr[j]
        top = lax.broadcasted_iota(jnp.int32, (SUBLANES, tn), 0)
    else:
        t = lax.broadcasted_iota(jnp.int32, (ch, tn), 0) & (seq_rows - 1)
    for ci in range(ROW_CHAINS):
        rows = slice(ci * ch, (ci + 1) * ch)
        h = h_scr[rows, :]
        a = _dot(h, wu)
        gate = _dot(h, wg)
        if seq_rows is None:
            a1 = pltpu.roll(a, 1, axis=0)
            a2 = pltpu.roll(a, 2, axis=0)
            act_ref[rows, :] = gated(conv(a2, a1, a), gate)
            first = slice(ci * ch, ci * ch + SUBLANES)
            p1 = above[SUBLANES - 1:SUBLANES, :]
            p2 = above[SUBLANES - 2:SUBLANES - 1, :]
            t1 = jnp.where(top == 0, p1, a1[:SUBLANES])
            t2 = jnp.where(top == 0, p2, jnp.where(top == 1, p1, a2[:SUBLANES]))
            act_ref[first, :] = gated(conv(t2, t1, a[:SUBLANES]), gate[:SUBLANES])
            above = a[ch - SUBLANES:, :]
        else:
            prev = prev_ref[rows, :]
            a1 = jnp.where(t == 0, pltpu.roll(prev, ch - 1, axis=0), pltpu.roll(a, 1, axis=0))
            a2 = jnp.where(t < 2, prev, pltpu.roll(a, 2, axis=0))
            st_ref[rows, :] = a
            act_ref[rows, :] = gated(conv(a2, a1, a), gate)
    if seq_rows is None:
        carry_scr[j] = above
        st_ref[...] = above


def _ffn_up(x, g2, wu, wg, cw, cb, tm, prev=None, seq_rows=None, tn=FFN_TN):
    m, d = x.shape
    f = wu.shape[1]
    nj = f // tn
    col = lambda i, j: (0, j)
    wtile = pl.BlockSpec((d, tn), col)
    x_mode = {"pipeline_mode": pl.Buffered(1)} if tm * d * 4 > VMEM_LIMIT // 4 else {}
    in_specs = [pl.BlockSpec((tm, d), lambda i, j: (i, 0), **x_mode),
                pl.BlockSpec((1, d), lambda i, j: (0, 0)),
                wtile,
                wtile,
                pl.BlockSpec((CONV_W, tn), col),
                pl.BlockSpec((1, tn), col)]
    args = [x, g2, wu, wg, cw, cb]
    scratch = [pltpu.VMEM((tm, d), BF16)]
    if seq_rows is None:
        st_shape = jax.ShapeDtypeStruct((m // tm * SUBLANES, f), F32)
        st_spec = pl.BlockSpec((SUBLANES, tn), lambda i, j: (i, j))
        scratch.append(pltpu.VMEM((nj, SUBLANES, tn), F32))
    else:
        in_specs.append(pl.BlockSpec((tm, tn), lambda i, j: (i, j)))
        args.append(prev)
        st_shape = jax.ShapeDtypeStruct((m, f), F32)
        st_spec = pl.BlockSpec((tm, tn), lambda i, j: (i, j))
    return pl.pallas_call(
        functools.partial(_ffn_up_body, seq_rows=seq_rows),
        grid=(m // tm, nj),
        in_specs=in_specs,
        out_specs=[pl.BlockSpec((tm, tn), lambda i, j: (i, j)), st_spec],
        out_shape=[jax.ShapeDtypeStruct((m, f), BF16), st_shape],
        scratch_shapes=scratch,
        compiler_params=_cparams(2),
        name="ffn_up",
    )(*args)


def _layer_dense_tail(x, ya, yb, sa, sb, wts, tm, tm_down, prev=None, seq_rows=None):
    mrg = _merge(ya, yb, wts["oa"], wts["ob"], sa, sb, tm)
    x1 = _resid_mm(mrg, wts["out"], x, tm, MERGE_TN, "out_proj")
    act, st = _ffn_up(x1, wts["g2"], wts["up"], wts["gate"], wts["cw"], wts["cb"], tm, prev, seq_rows)
    y = _resid_mm(act, wts["down"], x1, tm_down, DOWN_TN, "ffn_down")
    return y, st


def kernel(x_prompt, x_sample, cache_k, cache_v, cache_logf, state_conv, page_table, norm1_g, w_in, b_f,
           q_norm_g, k_norm_g, gm_norm_g, w_s, b_s, w_oa, w_ob, w_out, norm2_g, w_up, w_gate, conv_w, conv_b,
           w_down):
    depth = w_in.shape[0]
    assert depth == 1
    bp, sp, d = x_prompt.shape
    assert bp == 1
    bs, ts, _ = x_sample.shape
    assert ts == SUBLANES
    f = w_up.shape[-1]
    l = 0

    qkv_w = 3 * ATT_W
    wi = w_in[l]
    wq = wi[:, :qkv_w].astype(BF16)
    wr = wi[:, qkv_w + N_HEADS:].astype(BF16)
    wf = jnp.pad(wi[:, qkv_w:qkv_w + N_HEADS], ((0, 0), (0, LANES - N_HEADS))).astype(BF16)
    bfp = jnp.pad(b_f[l], (0, LANES - N_HEADS)).reshape(1, LANES)
    g1 = norm1_g[l].reshape(1, d)
    qg = q_norm_g[l].reshape(1, HEAD_DIM)
    kg = k_norm_g[l].reshape(1, HEAD_DIM)
    gg = gm_norm_g[l].reshape(1, GM_W)
    wts = {"oa": w_oa[l].astype(BF16), "ob": w_ob[l].astype(BF16), "out": w_out[l].astype(BF16),
           "g2": norm2_g[l].reshape(1, d), "up": w_up[l].astype(BF16), "gate": w_gate[l].astype(BF16),
           "cw": conv_w[l], "cb": conv_b[l].reshape(1, f), "down": w_down[l].astype(BF16)}
    ws = w_s[l].astype(BF16)

    xp = x_prompt.reshape(sp, d)
    q, kf, kb, vf, vb, gu, gv, sa, sb, lf = _inproj(xp, g1, wq, wr, wf, bfp, qg, kg, tm_norm=512, tm=1024)
    ya = _fox_prompt(q, kb, vb, _negcumsum(lf))
    rep = GMLP_CHUNKS_PER_STEP
    (yb,) = _gmlp(gu, gv, gg, jnp.tile(ws, (1, rep, rep)), jnp.tile(jnp.transpose(b_s[l]), (rep, 1)),
                  rep * CHUNK, CHUNK, False)
    yp, st_p = _layer_dense_tail(xp, ya, yb, sa, sb, wts, 1024, 512)

    ms = bs * ts
    xs = x_sample.reshape(ms, d)
    q, kf_s, kb_s, vf_s, vb_s, gu, gv, sa, sb, lf_s = _inproj(xs, g1, wq, wr, wf, bfp, qg, kg, tm_norm=ms, tm=ms)
    rows = N_HEADS * ts
    qcat = jnp.transpose(q.reshape(bs, ts, N_HEADS, HEAD_DIM), (0, 2, 1, 3)).reshape(bs, rows, HEAD_DIM)
    n_pool = cache_k.shape[1]
    page_rows = cache_k.shape[2] * N_HEADS
    ck = cache_k[l].reshape(n_pool, page_rows, HEAD_DIM)
    cv = cache_v[l].reshape(n_pool, page_rows, HEAD_DIM)
    clf = cache_logf[l].reshape(n_pool, page_rows // LANES, LANES)
    o = _fox_sample(page_table, qcat, ck, cv, clf,
                    kb_s.reshape(bs, rows, HEAD_DIM), vb_s.reshape(bs, rows, HEAD_DIM),
                    lf_s[:, :N_HEADS].reshape(bs, 1, rows))
    ya = jnp.transpose(o.reshape(bs, N_HEADS, ts, HEAD_DIM), (0, 2, 1, 3)).reshape(ms, ATT_W).astype(BF16)
    w_small = jnp.tile(ws[:, :ts, :ts], (1, bs, bs))
    bt_small = jnp.tile(jnp.transpose(b_s[l][:, :ts]), (bs, 1))
    yb, vn = _gmlp(gu, gv, gg, w_small, bt_small, ms, ts, True)
    prev = jnp.pad(state_conv[l], ((0, 0), (0, ts - (CONV_W - 1)), (0, 0))).reshape(ms, f)
    ys, a_s = _layer_dense_tail(xs, ya, yb, sa, sb, wts, ms, ms, prev, ts)

    return (yp.reshape(1, sp, d),
            ys.reshape(bs, ts, d),
            kf.reshape(1, 1, sp, N_HEADS, HEAD_DIM),
            vf.reshape(1, 1, sp, N_HEADS, HEAD_DIM),
            lf[:, :N_HEADS].reshape(1, 1, sp, N_HEADS),
            st_p[st_p.shape[0] - (CONV_W - 1):].reshape(1, 1, CONV_W - 1, f),
            kf_s.reshape(1, bs, ts, N_HEADS, HEAD_DIM),
            vf_s.reshape(1, bs, ts, N_HEADS, HEAD_DIM),
            lf_s[:, :N_HEADS].reshape(1, bs, ts, N_HEADS),
            vn.reshape(1, bs, ts, GM_W),
            a_s.reshape(bs, ts, f)[:, ts - (CONV_W - 1):].reshape(1, bs, CONV_W - 1, f))
```

```python
import functools

import jax
import jax.numpy as jnp
import numpy as np
from jax import lax
from jax.experimental import pallas as pl
from jax.experimental.pallas import tpu as pltpu

F32 = jnp.float32
BF16 = jnp.bfloat16

LANES = 128
SUBLANES = 8
VMEM_LIMIT = 56 * 1024 * 1024

HEAD_DIM = 128
N_HEADS = 16
ATT_W = N_HEADS * HEAD_DIM
GM_GROUPS = 16
GM_GROUP_W = 128
GM_W = GM_GROUPS * GM_GROUP_W
CHUNK = 128
CONV_W = 3
EPS = 1e-6
NEG = -1e30
SCALE = HEAD_DIM ** -0.5
LOG2E = float(np.log2(np.e))
Q_FOLD = SCALE * LOG2E

PROJ_TN = 512
MERGE_TN = 512
FFN_TN = 256
DOWN_TN = 512
GMLP_CHUNKS_PER_STEP = 4
PAGES_PER_STEP = 8


def _cparams(n_axes):
    return pltpu.CompilerParams(dimension_semantics=("arbitrary",) * n_axes, vmem_limit_bytes=VMEM_LIMIT)


def _dot(a, b):
    return jnp.dot(a, b, preferred_element_type=F32)


def _dot_nt(a, b):
    return lax.dot_general(a, b, (((1,), (1,)), ((), ())), preferred_element_type=F32)


def _rms_rows(x, g):
    return x * lax.rsqrt(jnp.mean(x * x, axis=-1, keepdims=True) + EPS) * g


def _sigmoid(x):
    return 0.5 + 0.5 * jnp.tanh(0.5 * x)


def _gelu_tanh(x):
    return 0.5 * x * (1.0 + jnp.tanh(np.sqrt(2.0 / np.pi).astype(np.float32) * (x + 0.044715 * (x * x * x))))


def _log_sigmoid(x):
    return jnp.minimum(x, 0.0) - jnp.log(1.0 + jnp.exp(-jnp.abs(x)))


ROW_CHAINS = 1


def _split_w_in_body(x_ref, tail_ref, oa_ref, or_ref, *, n_a, n_r, shift):
    oa_ref[...] = x_ref[:, :n_a * LANES].astype(BF16)
    lane = lax.broadcasted_iota(jnp.int32, (x_ref.shape[0], LANES), 1)
    base = (n_a - 1) * LANES

    def realign(lo, hi):
        return jnp.where(lane < LANES - shift, pltpu.roll(lo, LANES - shift, axis=1),
                         pltpu.roll(hi, LANES - shift, axis=1)).astype(BF16)

    for k in range(n_r):
        off = base + k * LANES
        hi = x_ref[:, off + LANES:off + 2 * LANES] if k < n_r - 1 else tail_ref[...]
        or_ref[:, k * LANES:(k + 1) * LANES] = realign(x_ref[:, off:off + LANES], hi)


def _split_w_in(w, n_first, n_skip, tr=128):
    k, n = w.shape
    n_a = n_first // LANES + 1
    n_rest = n - n_first - n_skip
    n_r = n_rest // LANES
    assert n_first % LANES == 0 and n_rest % LANES == 0 and 0 < n_skip < LANES
    tail = jnp.pad(w[:, n - n_skip:], ((0, 0), (0, LANES - n_skip)))
    return pl.pallas_call(
        functools.partial(_split_w_in_body, n_a=n_a, n_r=n_r, shift=n_skip),
        grid=(k // tr,),
        in_specs=[pl.BlockSpec((tr, n), lambda i: (i, 0)), pl.BlockSpec((tr, LANES), lambda i: (i, 0))],
        out_specs=[pl.BlockSpec((tr, n_a * LANES), lambda i: (i, 0)), pl.BlockSpec((tr, n_rest), lambda i: (i, 0))],
        out_shape=[jax.ShapeDtypeStruct((k, n_a * LANES), BF16), jax.ShapeDtypeStruct((k, n_rest), BF16)],
        compiler_params=_cparams(1),
        name="split_w_in",
    )(w, tail)


def _rmsnorm_body(x_ref, g_ref, o_ref):
    o_ref[...] = _rms_rows(x_ref[...], g_ref[...]).astype(BF16)


def _rmsnorm(x, g, tm):
    m, d = x.shape
    return pl.pallas_call(
        _rmsnorm_body,
        grid=(m // tm,),
        in_specs=[pl.BlockSpec((tm, d), lambda i: (i, 0)), pl.BlockSpec((1, d), lambda i: (0, 0))],
        out_specs=pl.BlockSpec((tm, d), lambda i: (i, 0)),
        out_shape=jax.ShapeDtypeStruct((m, d), BF16),
        compiler_params=_cparams(1),
        name="rmsnorm",
    )(x, g)


def _sections(h_ref, w_ref, table):
    j = pl.program_id(1)
    for lo, n, epilogue in table:
        @pl.when(jnp.logical_and(j >= lo, j < lo + n))
        def _(epilogue=epilogue):
            epilogue(_dot(h_ref[...], w_ref[...]))


_HEADS = [slice(h * HEAD_DIM, (h + 1) * HEAD_DIM) for h in range(PROJ_TN // HEAD_DIM)]
_SEC_QKV = {"q": (0, 4), "k": (4, 4), "v": (8, 4)}
_SEC_REST = {"u": (0, 4), "vg": (4, 4), "ga": (8, 8), "gb": (16, 8)}


def _proj_qkv_body(h_ref, w_ref, wf_ref, bf_ref, qg_ref, kg_ref, q_ref, kf_ref, kb_ref, vf_ref, vb_ref, lf_ref):
    @pl.when(pl.program_id(1) == 0)
    def _():
        lf_ref[...] = _log_sigmoid(_dot(h_ref[...], wf_ref[...]) + bf_ref[...])

    def q_epi(acc):
        for sl in _HEADS:
            q_ref[:, sl] = (_rms_rows(acc[:, sl], qg_ref[...]) * Q_FOLD).astype(BF16)

    def k_epi(acc):
        for sl in _HEADS:
            y = _rms_rows(acc[:, sl], kg_ref[...])
            kf_ref[:, sl] = y
            kb_ref[:, sl] = y.astype(BF16)

    def v_epi(acc):
        vf_ref[...] = acc
        vb_ref[...] = acc.astype(BF16)

    _sections(h_ref, w_ref, [(*_SEC_QKV["q"], q_epi), (*_SEC_QKV["k"], k_epi), (*_SEC_QKV["v"], v_epi)])


def _proj_rest_body(h_ref, w_ref, gu_ref, gv_ref, sa_ref, sb_ref):
    def u_epi(acc):
        gu_ref[...] = _gelu_tanh(acc).astype(BF16)

    def vg_epi(acc):
        gv_ref[...] = _gelu_tanh(acc)

    def ga_epi(acc):
        sa_ref[...] = _sigmoid(acc).astype(BF16)

    def gb_epi(acc):
        sb_ref[...] = _sigmoid(acc).astype(BF16)

    _sections(h_ref, w_ref, [(*_SEC_REST["u"], u_epi), (*_SEC_REST["vg"], vg_epi),
                             (*_SEC_REST["ga"], ga_epi), (*_SEC_REST["gb"], gb_epi)])


def _proj_call(body, name, h, w, extra, extra_specs, secs, outs, tm, tail_out=None):
    m, d = h.shape
    nj = w.shape[1] // PROJ_TN

    def sec(sname):
        lo, n = secs[sname]
        return pl.BlockSpec((tm, PROJ_TN), lambda i, j: (i, jnp.clip(j - lo, 0, n - 1)))

    out_specs = [sec(sname) for sname, _ in outs]
    out_shape = [jax.ShapeDtypeStruct((m, secs[sname][1] * PROJ_TN), dt) for sname, dt in outs]
    if tail_out:
        out_specs.append(pl.BlockSpec((tm, LANES), lambda i, j: (i, 0)))
        out_shape.append(jax.ShapeDtypeStruct((m, LANES), F32))
    return pl.pallas_call(
        body,
        grid=(m // tm, nj),
        in_specs=[pl.BlockSpec((tm, d), lambda i, j: (i, 0)),
                  pl.BlockSpec((d, PROJ_TN), lambda i, j: (0, j))] + extra_specs,
        out_specs=out_specs,
        out_shape=out_shape,
        compiler_params=_cparams(2),
        name=name,
    )(h, w, *extra)


def _inproj(x, g1, wq, wr, bfp, qg, kg, tm_norm, tm):
    d = x.shape[1]
    h = _rmsnorm(x, g1, tm_norm)
    const = lambda i, j: (0, 0)
    f_tile = wq.shape[1] // LANES - 1
    q, kf, kb, vf, vb, lf = _proj_call(
        _proj_qkv_body, "proj_qkv", h, wq, [wq, bfp, qg, kg],
        [pl.BlockSpec((d, LANES), lambda i, j: (0, f_tile)), pl.BlockSpec((1, LANES), const),
         pl.BlockSpec((1, HEAD_DIM), const), pl.BlockSpec((1, HEAD_DIM), const)],
        _SEC_QKV, [("q", BF16), ("k", F32), ("k", BF16), ("v", F32), ("v", BF16)], tm, tail_out=True)
    gu, gv, sa, sb = _proj_call(
        _proj_rest_body, "proj_rest", h, wr, [], [],
        _SEC_REST, [("u", BF16), ("vg", F32), ("ga", BF16), ("gb", BF16)], tm)
    return q, kf, kb, vf, vb, gu, gv, sa, sb, lf


def _lane_cumsum(x, shifts, lane):
    for sh in shifts:
        x = x + jnp.where(lane >= sh, pltpu.roll(x, sh, axis=1), 0.0)
    return x


def _negcumsum_body(lf_ref, o_ref):
    nblk = lf_ref.shape[0] // LANES
    lane = lax.broadcasted_iota(jnp.int32, (N_HEADS, LANES), 1)

    def body(b, carry):
        off = pl.multiple_of(b * LANES, LANES)
        xt = jnp.transpose(lf_ref[pl.ds(off, LANES), :])[:N_HEADS, :]
        c = _lane_cumsum(xt, (1, 2, 4, 8, 16, 32, 64), lane) + carry
        o_ref[:, pl.ds(off, LANES)] = c * (-LOG2E)
        return c[:, LANES - 1:LANES]

    lax.fori_loop(0, nblk, body, jnp.zeros((N_HEADS, 1), F32))


def _negcumsum(lf):
    s = lf.shape[0]
    return pl.pallas_call(
        _negcumsum_body,
        out_shape=jax.ShapeDtypeStruct((N_HEADS, s), F32),
        compiler_params=pltpu.CompilerParams(vmem_limit_bytes=VMEM_LIMIT),
        name="negcumsum",
    )(lf)


def _fox_prompt_body(q_ref, k_ref, v_ref, nc_ref, o_ref, m_scr, l_scr, acc_scr, *, tq, rq):
    qi = pl.program_id(1)
    nrc = tq // rq
    m_scr[...] = jnp.full(m_scr.shape, NEG, F32)
    l_scr[...] = jnp.zeros(l_scr.shape, F32)
    acc_scr[...] = jnp.zeros(acc_scr.shape, F32)

    def update(rc, zs, vs):
        rows = slice(rc * rq, (rc + 1) * rq)
        m_old = m_scr[rows, :]
        m_new = m_old
        for z in zs:
            m_new = jnp.maximum(m_new, jnp.max(z, axis=-1, keepdims=True))
        alpha = jnp.exp2(m_old - m_new)
        l_new = alpha * l_scr[rows, :]
        acc = alpha * acc_scr[rows, :]
        for z, v in zip(zs, vs):
            pr = jnp.exp2(z - m_new)
            l_new = l_new + jnp.sum(pr, axis=-1, keepdims=True)
            acc = acc + _dot(pr.astype(BF16), v)
        l_scr[rows, :] = l_new
        acc_scr[rows, :] = acc
        m_scr[rows, :] = m_new

    def scores(rc, start, n):
        return (_dot_nt(q_ref[rc * rq:(rc + 1) * rq, :], k_ref[pl.ds(start, n), :]) + nc_ref[:, pl.ds(start, n)])

    def below_diagonal(ki, carry):
        off = pl.multiple_of(ki * tq, tq)
        v = v_ref[pl.ds(off, tq), :]
        for rc in range(nrc):
            update(rc, [scores(rc, off, tq)], [v])
        return carry

    lax.fori_loop(0, qi, below_diagonal, 0)

    off = pl.multiple_of(qi * tq, tq)
    causal = (lax.broadcasted_iota(jnp.int32, (rq, rq), 0) >= lax.broadcasted_iota(jnp.int32, (rq, rq), 1))
    for rc in range(nrc):
        n = rc * rq
        dstart = pl.multiple_of(off + n, rq)
        zs = [jnp.where(causal, scores(rc, dstart, rq), NEG)]
        vs = [v_ref[pl.ds(dstart, rq), :]]
        if n:
            zs.append(scores(rc, off, n))
            vs.append(v_ref[pl.ds(off, n), :])
        update(rc, zs, vs)
    o_ref[...] = (acc_scr[...] / l_scr[...]).astype(o_ref.dtype)


def _fox_prompt(qb, kb, vb, negc, tq=2048, rq=256):
    s = qb.shape[0]
    head = lambda h, i: (0, h)
    return pl.pallas_call(
        functools.partial(_fox_prompt_body, tq=tq, rq=rq),
        grid=(N_HEADS, s // tq),
        in_specs=[pl.BlockSpec((tq, HEAD_DIM), lambda h, i: (i, h)),
                  pl.BlockSpec((s, HEAD_DIM), head),
                  pl.BlockSpec((s, HEAD_DIM), head),
                  pl.BlockSpec((None, 1, s), lambda h, i: (h, 0, 0))],
        out_specs=pl.BlockSpec((tq, HEAD_DIM), lambda h, i: (i, h)),
        out_shape=jax.ShapeDtypeStruct((s, ATT_W), BF16),
        scratch_shapes=[pltpu.VMEM((tq, 1), F32), pltpu.VMEM((tq, 1), F32), pltpu.VMEM((tq, HEAD_DIM), F32)],
        compiler_params=_cparams(2),
        name="fox_prompt",
    )(qb, kb, vb, negc.reshape(N_HEADS, 1, s))


def _page_negcumsum(lf, run, lane, rowj):
    x = _lane_cumsum(lf, (16, 32, 64), lane)
    t = jnp.where(lane >= LANES - N_HEADS, x, 0.0)
    for sh in (16, 32, 64):
        t = t + pltpu.roll(t, sh, axis=1)
    e = t
    for sh in (1, 2, 4, 8):
        e = e + jnp.where(rowj >= sh, pltpu.roll(e, sh, axis=0), 0.0)
    c = x + (e - t) + run
    return -c, run + e[N_HEADS - 1:N_HEADS, :]


def _fox_sample_body(pt_ref, q_ref, *refs, npg):
    del pt_ref
    k_refs, v_refs, lf_refs = refs[:npg], refs[npg:2 * npg], refs[2 * npg:3 * npg]
    kn_ref, vn_ref, lfn_ref, o_ref, m_scr, l_scr, acc_scr, run_scr = refs[3 * npg:]
    g = pl.program_id(1)
    tiles = k_refs[0].shape[0] // LANES

    @pl.when(g == 0)
    def _():
        m_scr[...] = jnp.full(m_scr.shape, NEG, F32)
        l_scr[...] = jnp.zeros(l_scr.shape, F32)
        acc_scr[...] = jnp.zeros(acc_scr.shape, F32)
        run_scr[...] = jnp.zeros(run_scr.shape, F32)

    lane = lax.broadcasted_iota(jnp.int32, (SUBLANES, LANES), 1)
    sub = lax.broadcasted_iota(jnp.int32, (SUBLANES, LANES), 0)
    head_is = [(lane & (N_HEADS - 1)) == h for h in range(N_HEADS)]
    lane16 = lax.broadcasted_iota(jnp.int32, (N_HEADS, LANES), 1)
    row16 = lax.broadcasted_iota(jnp.int32, (N_HEADS, LANES), 0)
    q = q_ref[...]

    def compact(tile):
        zc = tile[(N_HEADS - 1) * SUBLANES:, :]
        for h in range(N_HEADS - 2, -1, -1):
            zc = jnp.where(head_is[h], tile[h * SUBLANES:(h + 1) * SUBLANES, :], zc)
        return zc

    def spread(pc):
        return jnp.concatenate([jnp.where(head_is[h], pc, 0.0) for h in range(N_HEADS)], axis=0)

    def over_tokens(x, op):
        for sh in (16, 32, 64):
            x = op(x, pltpu.roll(x, sh, axis=1))
        return x

    def per_row(x):
        return jnp.concatenate([jnp.broadcast_to(x[:, h:h + 1], (SUBLANES, LANES)) for h in range(N_HEADS)], axis=0)

    def attend(zcs, vbs, m, l, acc):
        zmax = zcs[0]
        for z in zcs[1:]:
            zmax = jnp.maximum(zmax, z)
        m_new = jnp.maximum(m, over_tokens(zmax, jnp.maximum))
        alpha = jnp.exp2(m - m_new)
        per = len(zcs) // len(vbs)
        psum = None
        pv = None
        for i, vb in enumerate(vbs):
            ps = [jnp.exp2(z - m_new) for z in zcs[i * per:(i + 1) * per]]
            for pp in ps:
                psum = pp if psum is None else psum + pp
            pm = jnp.concatenate([spread(pp) for pp in ps], axis=1) if per > 1 else spread(ps[0])
            d = _dot(pm.astype(BF16), vb)
            pv = d if pv is None else pv + d
        l = alpha * l + over_tokens(psum, jnp.add)
        return m_new, l, per_row(alpha) * acc + pv

    run = run_scr[...]
    zcs = []
    for p in range(npg):
        gt = _dot_nt(q, k_refs[p][...].astype(BF16))
        negc, run = _page_negcumsum(lf_refs[p][...], run, lane16, row16)
        negc = negc * LOG2E
        zcs += [compact(gt[:, t * LANES:(t + 1) * LANES]) + negc[t:t + 1, :] for t in range(tiles)]
    m, l, acc = attend(zcs, [v_refs[p][...].astype(BF16) for p in range(npg)],
                       m_scr[...], l_scr[...], acc_scr[...])
    m_scr[...], l_scr[...], acc_scr[...], run_scr[...] = m, l, acc, run

    @pl.when(g == pl.num_programs(1) - 1)
    def _():
        cn = _lane_cumsum(lfn_ref[...], (16, 32, 64), lane16[:1]) + run
        zn = compact(_dot_nt(q, kn_ref[...])) - cn * LOG2E
        _, l2, acc2 = attend([jnp.where((lane >> 4) <= sub, zn, NEG)], [vn_ref[...]], m, l, acc)
        o_ref[...] = acc2 / per_row(l2)


def _fox_sample(page_table, qcat, ck, cv, clf, kn, vn, lfn):
    nb = qcat.shape[0]
    n_pages = page_table.shape[1]
    npg = PAGES_PER_STEP
    rows = N_HEADS * SUBLANES
    prow = ck.shape[1]

    def page(p):
        return lambda b, g, pt: (pt[b * n_pages + g * npg + p], 0, 0)

    per_b = lambda b, g, pt: (b, 0, 0)
    in_specs = ([pl.BlockSpec((None, rows, HEAD_DIM), per_b)]
                + [pl.BlockSpec((None, prow, HEAD_DIM), page(p)) for p in range(npg)]
                + [pl.BlockSpec((None, prow, HEAD_DIM), page(p)) for p in range(npg)]
                + [pl.BlockSpec((None, N_HEADS, LANES), page(p)) for p in range(npg)]
                + [pl.BlockSpec((None, rows, HEAD_DIM), per_b),
                   pl.BlockSpec((None, rows, HEAD_DIM), per_b),
                   pl.BlockSpec((None, 1, LANES), per_b)])
    grid_spec = pltpu.PrefetchScalarGridSpec(
        num_scalar_prefetch=1,
        grid=(nb, n_pages // npg),
        in_specs=in_specs,
        out_specs=pl.BlockSpec((None, rows, HEAD_DIM), per_b),
        scratch_shapes=[pltpu.VMEM((SUBLANES, LANES), F32), pltpu.VMEM((SUBLANES, LANES), F32),
                        pltpu.VMEM((rows, HEAD_DIM), F32), pltpu.VMEM((1, LANES), F32)],
    )
    return pl.pallas_call(
        functools.partial(_fox_sample_body, npg=npg),
        grid_spec=grid_spec,
        out_shape=jax.ShapeDtypeStruct((nb, rows, HEAD_DIM), F32),
        compiler_params=_cparams(2),
        name="fox_sample",
    )(page_table.reshape(-1), qcat, *([ck] * npg), *([cv] * npg), *([clf] * npg), kn, vn, lfn)


def _gmlp_body(gu_ref, gv_ref, gg_ref, w_ref, bt_ref, yb_ref, *vn_ref, blk):
    r = gv_ref.shape[0]
    vn = _rms_rows(gv_ref[...], gg_ref[...])
    if vn_ref:
        vn_ref[0][...] = vn
    vnb = vn.astype(BF16)
    rr = lax.broadcasted_iota(jnp.int32, (r, r), 0)
    cc = lax.broadcasted_iota(jnp.int32, (r, r), 1)
    sh = blk.bit_length() - 1
    keep = jnp.logical_and((rr >> sh) == (cc >> sh), (cc & (blk - 1)) <= (rr & (blk - 1)))
    for g in range(GM_GROUPS):
        sl = slice(g * GM_GROUP_W, (g + 1) * GM_GROUP_W)
        w = jnp.where(keep, w_ref[g], jnp.zeros((), BF16))
        mixed = _dot(w, vnb[:, sl]) + bt_ref[:, g:g + 1]
        yb_ref[:, sl] = (gu_ref[:, sl].astype(F32) * mixed).astype(BF16)


def _gmlp(gu, gv, gg, w, bt, rows, blk, want_vn):
    m = gu.shape[0]
    row_spec = pl.BlockSpec((rows, GM_W), lambda i: (i, 0))
    out_shape = [jax.ShapeDtypeStruct((m, GM_W), BF16)]
    out_specs = [row_spec]
    if want_vn:
        out_shape.append(jax.ShapeDtypeStruct((m, GM_W), F32))
        out_specs.append(row_spec)
    return pl.pallas_call(
        functools.partial(_gmlp_body, blk=blk),
        grid=(m // rows,),
        in_specs=[row_spec, row_spec,
                  pl.BlockSpec((1, GM_W), lambda i: (0, 0)),
                  pl.BlockSpec((GM_GROUPS, rows, rows), lambda i: (0, 0, 0)),
                  pl.BlockSpec((rows, GM_GROUPS), lambda i: (0, 0))],
        out_specs=out_specs,
        out_shape=out_shape,
        compiler_params=_cparams(1),
        name="gmlp",
    )(gu, gv, gg, w, bt)


def _merge_body(ya_ref, yb_ref, woa_ref, wob_ref, sa_ref, sb_ref, m_ref):
    a = _dot(ya_ref[...], woa_ref[...])
    b = _dot(yb_ref[...], wob_ref[...])
    m_ref[...] = (sa_ref[...].astype(F32) * a + sb_ref[...].astype(F32) * b).astype(BF16)


def _merge(ya, yb, woa, wob, sa, sb, tm, tn=MERGE_TN):
    m, ka = ya.shape
    n = woa.shape[1]
    nt = n // tn
    wtile = pl.BlockSpec((ka, tn), lambda i, j: (0, j))
    return pl.pallas_call(
        _merge_body,
        grid=(m // tm, nt),
        in_specs=[pl.BlockSpec((tm, ka), lambda i, j: (i, 0)),
                  pl.BlockSpec((tm, ka), lambda i, j: (i, 0)),
                  wtile,
                  wtile,
                  pl.BlockSpec((tm, tn), lambda i, j: (i, j)),
                  pl.BlockSpec((tm, tn), lambda i, j: (i, j))],
        out_specs=pl.BlockSpec((tm, tn), lambda i, j: (i, j)),
        out_shape=jax.ShapeDtypeStruct((m, n), BF16),
        compiler_params=_cparams(2),
        name="merge",
    )(ya, yb, woa, wob, sa, sb)


def _resid_mm_body(a_ref, w_ref, x_ref, o_ref):
    o_ref[...] = x_ref[...] + _dot(a_ref[...], w_ref[...])


def _resid_mm(a, w, x, tm, tn, name):
    m, k = a.shape
    n = w.shape[1]
    return pl.pallas_call(
        _resid_mm_body,
        grid=(m // tm, n // tn),
        in_specs=[pl.BlockSpec((tm, k), lambda i, j: (i, 0)),
                  pl.BlockSpec((k, tn), lambda i, j: (0, j)),
                  pl.BlockSpec((tm, tn), lambda i, j: (i, j))],
        out_specs=pl.BlockSpec((tm, tn), lambda i, j: (i, j)),
        out_shape=jax.ShapeDtypeStruct((m, n), F32),
        compiler_params=_cparams(2),
        name=name,
    )(a, w, x)


def _ffn_up_body(x_ref, g2_ref, wu_ref, wg_ref, cw_ref, cb_ref, *refs, seq_rows):
    i = pl.program_id(0)
    j = pl.program_id(1)
    if seq_rows is None:
        act_ref, st_ref, h_scr, carry_scr = refs
    else:
        prev_ref, act_ref, st_ref, h_scr = refs

    @pl.when(j == 0)
    def _():
        h_scr[...] = _rms_rows(x_ref[...], g2_ref[...]).astype(BF16)

    ch = h_scr.shape[0] // ROW_CHAINS
    wu = wu_ref[...]
    wg = wg_ref[...]
    tn = wu.shape[1]

    def gated(c, g):
        hc = 0.5 * c
        return ((hc + hc * jnp.tanh(hc)) * g).astype(BF16)

    def conv(x2, x1, x0):
        return cb_ref[...] + cw_ref[0:1, :] * x2 + cw_ref[1:2, :] * x1 + cw_ref[2:3, :] * x0

    if seq_rows is None:
        @pl.when(i == 0)
        def _():
            carry_scr[j] = jnp.zeros(carry_scr.shape[1:], F32)

        above = carry_scr[j]
        top = lax.broadcasted_iota(jnp.int32, (SUBLANES, tn), 0)
    else:
        t = lax.broadcasted_iota(jnp.int32, (ch, tn), 0) & (seq_rows - 1)
    for ci in range(ROW_CHAINS):
        rows = slice(ci * ch, (ci + 1) * ch)
        h = h_scr[rows, :]
        a = _dot(h, wu)
        gate = _dot(h, wg)
        if seq_rows is None:
            a1 = pltpu.roll(a, 1, axis=0)
            a2 = pltpu.roll(a, 2, axis=0)
            act_ref[rows, :] = gated(conv(a2, a1, a), gate)
            first = slice(ci * ch, ci * ch + SUBLANES)
            p1 = above[SUBLANES - 1:SUBLANES, :]
            p2 = above[SUBLANES - 2:SUBLANES - 1, :]
            t1 = jnp.where(top == 0, p1, a1[:SUBLANES])
            t2 = jnp.where(top == 0, p2, jnp.where(top == 1, p1, a2[:SUBLANES]))
            act_ref[first, :] = gated(conv(t2, t1, a[:SUBLANES]), gate[:SUBLANES])
            above = a[ch - SUBLANES:, :]
        else:
            prev = prev_ref[rows, :]
            a1 = jnp.where(t == 0, pltpu.roll(prev, ch - 1, axis=0), pltpu.roll(a, 1, axis=0))
            a2 = jnp.where(t < 2, prev, pltpu.roll(a, 2, axis=0))
            st_ref[rows, :] = a
            act_ref[rows, :] = gated(conv(a2, a1, a), gate)
    if seq_rows is None:
        carry_scr[j] = above
        st_ref[...] = above


def _ffn_up(x, g2, wu, wg, cw, cb, tm, prev=None, seq_rows=None, tn=FFN_TN):
    m, d = x.shape
    f = wu.shape[1]
    nj = f // tn
    col = lambda i, j: (0, j)
    wtile = pl.BlockSpec((d, tn), col)
    x_mode = {"pipeline_mode": pl.Buffered(1)} if tm * d * 4 > VMEM_LIMIT // 4 else {}
    in_specs = [pl.BlockSpec((tm, d), lambda i, j: (i, 0), **x_mode),
                pl.BlockSpec((1, d), lambda i, j: (0, 0)),
                wtile,
                wtile,
                pl.BlockSpec((CONV_W, tn), col),
                pl.BlockSpec((1, tn), col)]
    args = [x, g2, wu, wg, cw, cb]
    scratch = [pltpu.VMEM((tm, d), BF16)]
    if seq_rows is None:
        st_shape = jax.ShapeDtypeStruct((m // tm * SUBLANES, f), F32)
        st_spec = pl.BlockSpec((SUBLANES, tn), lambda i, j: (i, j))
        scratch.append(pltpu.VMEM((nj, SUBLANES, tn), F32))
    else:
        in_specs.append(pl.BlockSpec((tm, tn), lambda i, j: (i, j)))
        args.append(prev)
        st_shape = jax.ShapeDtypeStruct((m, f), F32)
        st_spec = pl.BlockSpec((tm, tn), lambda i, j: (i, j))
    return pl.pallas_call(
        functools.partial(_ffn_up_body, seq_rows=seq_rows),
        grid=(m // tm, nj),
        in_specs=in_specs,
        out_specs=[pl.BlockSpec((tm, tn), lambda i, j: (i, j)), st_spec],
        out_shape=[jax.ShapeDtypeStruct((m, f), BF16), st_shape],
        scratch_shapes=scratch,
        compiler_params=_cparams(2),
        name="ffn_up",
    )(*args)


def _layer_dense_tail(x, ya, yb, sa, sb, wts, tm, tm_down, prev=None, seq_rows=None):
    mrg = _merge(ya, yb, wts["oa"], wts["ob"], sa, sb, tm)
    x1 = _resid_mm(mrg, wts["out"], x, tm, MERGE_TN, "out_proj")
    act, st = _ffn_up(x1, wts["g2"], wts["up"], wts["gate"], wts["cw"], wts["cb"], tm, prev, seq_rows)
    y = _resid_mm(act, wts["down"], x1, tm_down, DOWN_TN, "ffn_down")
    return y, st


def kernel(x_prompt, x_sample, cache_k, cache_v, cache_logf, state_conv, page_table, norm1_g, w_in, b_f,
           q_norm_g, k_norm_g, gm_norm_g, w_s, b_s, w_oa, w_ob, w_out, norm2_g, w_up, w_gate, conv_w, conv_b,
           w_down):
    depth = w_in.shape[0]
    assert depth == 1
    bp, sp, d = x_prompt.shape
    assert bp == 1
    bs, ts, _ = x_sample.shape
    assert ts == SUBLANES
    f = w_up.shape[-1]
    l = 0

    wq, wr = _split_w_in(w_in[l], 3 * ATT_W, N_HEADS)
    bfp = jnp.pad(b_f[l], (0, LANES - N_HEADS)).reshape(1, LANES)
    g1 = norm1_g[l].reshape(1, d)
    qg = q_norm_g[l].reshape(1, HEAD_DIM)
    kg = k_norm_g[l].reshape(1, HEAD_DIM)
    gg = gm_norm_g[l].reshape(1, GM_W)
    wts = {"oa": w_oa[l].astype(BF16), "ob": w_ob[l].astype(BF16), "out": w_out[l].astype(BF16),
           "g2": norm2_g[l].reshape(1, d), "up": w_up[l].astype(BF16), "gate": w_gate[l].astype(BF16),
           "cw": conv_w[l], "cb": conv_b[l].reshape(1, f), "down": w_down[l].astype(BF16)}
    ws = w_s[l].astype(BF16)

    xp = x_prompt.reshape(sp, d)
    q, kf, kb, vf, vb, gu, gv, sa, sb, lf = _inproj(xp, g1, wq, wr, bfp, qg, kg, tm_norm=512, tm=1024)
    ya = _fox_prompt(q, kb, vb, _negcumsum(lf))
    rep = GMLP_CHUNKS_PER_STEP
    (yb,) = _gmlp(gu, gv, gg, jnp.tile(ws, (1, rep, rep)), jnp.tile(jnp.transpose(b_s[l]), (rep, 1)),
                  rep * CHUNK, CHUNK, False)
    yp, st_p = _layer_dense_tail(xp, ya, yb, sa, sb, wts, 1024, 512)

    ms = bs * ts
    xs = x_sample.reshape(ms, d)
    q, kf_s, kb_s, vf_s, vb_s, gu, gv, sa, sb, lf_s = _inproj(xs, g1, wq, wr, bfp, qg, kg, tm_norm=ms, tm=ms)
    rows = N_HEADS * ts
    qcat = jnp.transpose(q.reshape(bs, ts, N_HEADS, HEAD_DIM), (0, 2, 1, 3)).reshape(bs, rows, HEAD_DIM)
    n_pool = cache_k.shape[1]
    page_rows = cache_k.shape[2] * N_HEADS
    ck = cache_k[l].reshape(n_pool, page_rows, HEAD_DIM)
    cv = cache_v[l].reshape(n_pool, page_rows, HEAD_DIM)
    clf = cache_logf[l].reshape(n_pool, page_rows // LANES, LANES)
    o = _fox_sample(page_table, qcat, ck, cv, clf,
                    kb_s.reshape(bs, rows, HEAD_DIM), vb_s.reshape(bs, rows, HEAD_DIM),
                    lf_s[:, :N_HEADS].reshape(bs, 1, rows))
    ya = jnp.transpose(o.reshape(bs, N_HEADS, ts, HEAD_DIM), (0, 2, 1, 3)).reshape(ms, ATT_W).astype(BF16)
    w_small = jnp.tile(ws[:, :ts, :ts], (1, bs, bs))
    bt_small = jnp.tile(jnp.transpose(b_s[l][:, :ts]), (bs, 1))
    yb, vn = _gmlp(gu, gv, gg, w_small, bt_small, ms, ts, True)
    prev = jnp.pad(state_conv[l], ((0, 0), (0, ts - (CONV_W - 1)), (0, 0))).reshape(ms, f)
    ys, a_s = _layer_dense_tail(xs, ya, yb, sa, sb, wts, ms, ms, prev, ts)

    return (yp.reshape(1, sp, d),
            ys.reshape(bs, ts, d),
            kf.reshape(1, 1, sp, N_HEADS, HEAD_DIM),
            vf.reshape(1, 1, sp, N_HEADS, HEAD_DIM),
            lf[:, :N_HEADS].reshape(1, 1, sp, N_HEADS),
            st_p[st_p.shape[0] - (CONV_W - 1):].reshape(1, 1, CONV_W - 1, f),
            kf_s.reshape(1, bs, ts, N_HEADS, HEAD_DIM),
            vf_s.reshape(1, bs, ts, N_HEADS, HEAD_DIM),
            lf_s[:, :N_HEADS].reshape(1, bs, ts, N_HEADS),
            vn.reshape(1, bs, ts, GM_W),
            a_s.reshape(bs, ts, f)[:, ts - (CONV_W - 1):].reshape(1, bs, CONV_W - 1, f))
```

```python
import functools

import jax
import jax.numpy as jnp
import numpy as np
from jax import lax
from jax.experimental import pallas as pl
from jax.experimental.pallas import tpu as pltpu

F32 = jnp.float32
BF16 = jnp.bfloat16

LANES = 128
SUBLANES = 8
VMEM_LIMIT = 56 * 1024 * 1024

HEAD_DIM = 128
N_HEADS = 16
ATT_W = N_HEADS * HEAD_DIM
GM_GROUPS = 16
GM_GROUP_W = 128
GM_W = GM_GROUPS * GM_GROUP_W
CHUNK = 128
CONV_W = 3
EPS = 1e-6
NEG = -1e30
SCALE = HEAD_DIM ** -0.5
LOG2E = float(np.log2(np.e))
Q_FOLD = SCALE * LOG2E

PROJ_TN = 512
MERGE_TN = 512
FFN_TN = 256
DOWN_TN = 512
GMLP_CHUNKS_PER_STEP = 4
PAGES_PER_STEP = 8


def _cparams(n_axes):
    return pltpu.CompilerParams(dimension_semantics=("arbitrary",) * n_axes, vmem_limit_bytes=VMEM_LIMIT)


def _dot(a, b):
    return jnp.dot(a, b, preferred_element_type=F32)


def _dot_nt(a, b):
    return lax.dot_general(a, b, (((1,), (1,)), ((), ())), preferred_element_type=F32)


def _rms_rows(x, g):
    return x * lax.rsqrt(jnp.mean(x * x, axis=-1, keepdims=True) + EPS) * g


def _sigmoid(x):
    return 0.5 + 0.5 * jnp.tanh(0.5 * x)


def _gelu_tanh(x):
    return 0.5 * x * (1.0 + jnp.tanh(np.sqrt(2.0 / np.pi).astype(np.float32) * (x + 0.044715 * (x * x * x))))


def _log_sigmoid(x):
    return jnp.minimum(x, 0.0) - jnp.log(1.0 + jnp.exp(-jnp.abs(x)))


ROW_CHAINS = 1


def _rmsnorm_body(x_ref, g_ref, o_ref):
    o_ref[...] = _rms_rows(x_ref[...], g_ref[...]).astype(BF16)


def _rmsnorm(x, g, tm):
    m, d = x.shape
    return pl.pallas_call(
        _rmsnorm_body,
        grid=(m // tm,),
        in_specs=[pl.BlockSpec((tm, d), lambda i: (i, 0)), pl.BlockSpec((1, d), lambda i: (0, 0))],
        out_specs=pl.BlockSpec((tm, d), lambda i: (i, 0)),
        out_shape=jax.ShapeDtypeStruct((m, d), BF16),
        compiler_params=_cparams(1),
        name="rmsnorm",
    )(x, g)


def _sections(h_ref, w_ref, table):
    j = pl.program_id(1)
    for lo, n, epilogue in table:
        @pl.when(jnp.logical_and(j >= lo, j < lo + n))
        def _(epilogue=epilogue):
            epilogue(_dot_nt(h_ref[...], w_ref[...]))


_HEADS = [slice(h * HEAD_DIM, (h + 1) * HEAD_DIM) for h in range(PROJ_TN // HEAD_DIM)]
_SEC_QKV = {"q": (0, 4), "k": (4, 4), "v": (8, 4)}
_SEC_REST = {"u": (0, 4), "vg": (4, 4), "ga": (8, 8), "gb": (16, 8)}


def _proj_qkv_body(h_ref, w_ref, wf_ref, bf_ref, qg_ref, kg_ref, q_ref, kf_ref, kb_ref, vf_ref, vb_ref, lf_ref):
    @pl.when(pl.program_id(1) == 0)
    def _():
        lf_ref[...] = _log_sigmoid(_dot_nt(h_ref[...], wf_ref[...]) + bf_ref[...])

    def q_epi(acc):
        for sl in _HEADS:
            q_ref[:, sl] = (_rms_rows(acc[:, sl], qg_ref[...]) * Q_FOLD).astype(BF16)

    def k_epi(acc):
        for sl in _HEADS:
            y = _rms_rows(acc[:, sl], kg_ref[...])
            kf_ref[:, sl] = y
            kb_ref[:, sl] = y.astype(BF16)

    def v_epi(acc):
        vf_ref[...] = acc
        vb_ref[...] = acc.astype(BF16)

    _sections(h_ref, w_ref, [(*_SEC_QKV["q"], q_epi), (*_SEC_QKV["k"], k_epi), (*_SEC_QKV["v"], v_epi)])


def _proj_rest_body(h_ref, w_ref, gu_ref, gv_ref, sa_ref, sb_ref):
    def u_epi(acc):
        gu_ref[...] = _gelu_tanh(acc).astype(BF16)

    def vg_epi(acc):
        gv_ref[...] = _gelu_tanh(acc)

    def ga_epi(acc):
        sa_ref[...] = _sigmoid(acc).astype(BF16)

    def gb_epi(acc):
        sb_ref[...] = _sigmoid(acc).astype(BF16)

    _sections(h_ref, w_ref, [(*_SEC_REST["u"], u_epi), (*_SEC_REST["vg"], vg_epi),
                             (*_SEC_REST["ga"], ga_epi), (*_SEC_REST["gb"], gb_epi)])


def _proj_call(body, name, h, w, extra, extra_specs, secs, outs, tm, tail_out=None):
    m, d = h.shape
    nj = sum(n for _, n in secs.values())

    def sec(sname):
        lo, n = secs[sname]
        return pl.BlockSpec((tm, PROJ_TN), lambda i, j: (i, jnp.clip(j - lo, 0, n - 1)))

    out_specs = [sec(sname) for sname, _ in outs]
    out_shape = [jax.ShapeDtypeStruct((m, secs[sname][1] * PROJ_TN), dt) for sname, dt in outs]
    if tail_out:
        out_specs.append(pl.BlockSpec((tm, LANES), lambda i, j: (i, 0)))
        out_shape.append(jax.ShapeDtypeStruct((m, LANES), F32))
    return pl.pallas_call(
        body,
        grid=(m // tm, nj),
        in_specs=[pl.BlockSpec((tm, d), lambda i, j: (i, 0)),
                  pl.BlockSpec((PROJ_TN, d), lambda i, j: (j, 0))] + extra_specs,
        out_specs=out_specs,
        out_shape=out_shape,
        compiler_params=_cparams(2),
        name=name,
    )(h, w, *extra)


def _inproj(x, g1, wt, wr, bfp, qg, kg, tm_norm, tm):
    d = x.shape[1]
    h = _rmsnorm(x, g1, tm_norm)
    const = lambda i, j: (0, 0)
    f_block = sum(n for _, n in _SEC_QKV.values()) * PROJ_TN // LANES
    q, kf, kb, vf, vb, lf = _proj_call(
        _proj_qkv_body, "proj_qkv", h, wt, [wt, bfp, qg, kg],
        [pl.BlockSpec((LANES, d), lambda i, j: (f_block, 0)), pl.BlockSpec((1, LANES), const),
         pl.BlockSpec((1, HEAD_DIM), const), pl.BlockSpec((1, HEAD_DIM), const)],
        _SEC_QKV, [("q", BF16), ("k", F32), ("k", BF16), ("v", F32), ("v", BF16)], tm, tail_out=True)
    gu, gv, sa, sb = _proj_call(
        _proj_rest_body, "proj_rest", h, wr, [], [],
        _SEC_REST, [("u", BF16), ("vg", F32), ("ga", BF16), ("gb", BF16)], tm)
    return q, kf, kb, vf, vb, gu, gv, sa, sb, lf


def _lane_cumsum(x, shifts, lane):
    for sh in shifts:
        x = x + jnp.where(lane >= sh, pltpu.roll(x, sh, axis=1), 0.0)
    return x


def _negcumsum_body(lf_ref, o_ref):
    nblk = lf_ref.shape[0] // LANES
    lane = lax.broadcasted_iota(jnp.int32, (N_HEADS, LANES), 1)

    def body(b, carry):
        off = pl.multiple_of(b * LANES, LANES)
        xt = jnp.transpose(lf_ref[pl.ds(off, LANES), :])[:N_HEADS, :]
        c = _lane_cumsum(xt, (1, 2, 4, 8, 16, 32, 64), lane) + carry
        o_ref[:, pl.ds(off, LANES)] = c * (-LOG2E)
        return c[:, LANES - 1:LANES]

    lax.fori_loop(0, nblk, body, jnp.zeros((N_HEADS, 1), F32))


def _negcumsum(lf):
    s = lf.shape[0]
    return pl.pallas_call(
        _negcumsum_body,
        out_shape=jax.ShapeDtypeStruct((N_HEADS, s), F32),
        compiler_params=pltpu.CompilerParams(vmem_limit_bytes=VMEM_LIMIT),
        name="negcumsum",
    )(lf)


def _fox_prompt_body(q_ref, k_ref, v_ref, nc_ref, o_ref, m_scr, l_scr, acc_scr, *, tq, rq):
    qi = pl.program_id(1)
    nrc = tq // rq
    m_scr[...] = jnp.full(m_scr.shape, NEG, F32)
    l_scr[...] = jnp.zeros(l_scr.shape, F32)
    acc_scr[...] = jnp.zeros(acc_scr.shape, F32)

    def update(rc, zs, vs):
        rows = slice(rc * rq, (rc + 1) * rq)
        m_old = m_scr[rows, :]
        m_new = m_old
        for z in zs:
            m_new = jnp.maximum(m_new, jnp.max(z, axis=-1, keepdims=True))
        alpha = jnp.exp2(m_old - m_new)
        l_new = alpha * l_scr[rows, :]
        acc = alpha * acc_scr[rows, :]
        for z, v in zip(zs, vs):
            pr = jnp.exp2(z - m_new)
            l_new = l_new + jnp.sum(pr, axis=-1, keepdims=True)
            acc = acc + _dot(pr.astype(BF16), v)
        l_scr[rows, :] = l_new
        acc_scr[rows, :] = acc
        m_scr[rows, :] = m_new

    def scores(rc, start, n):
        return (_dot_nt(q_ref[rc * rq:(rc + 1) * rq, :], k_ref[pl.ds(start, n), :]) + nc_ref[:, pl.ds(start, n)])

    def below_diagonal(ki, carry):
        off = pl.multiple_of(ki * tq, tq)
        v = v_ref[pl.ds(off, tq), :]
        for rc in range(nrc):
            update(rc, [scores(rc, off, tq)], [v])
        return carry

    lax.fori_loop(0, qi, below_diagonal, 0)

    off = pl.multiple_of(qi * tq, tq)
    causal = (lax.broadcasted_iota(jnp.int32, (rq, rq), 0) >= lax.broadcasted_iota(jnp.int32, (rq, rq), 1))
    for rc in range(nrc):
        n = rc * rq
        dstart = pl.multiple_of(off + n, rq)
        zs = [jnp.where(causal, scores(rc, dstart, rq), NEG)]
        vs = [v_ref[pl.ds(dstart, rq), :]]
        if n:
            zs.append(scores(rc, off, n))
            vs.append(v_ref[pl.ds(off, n), :])
        update(rc, zs, vs)
    o_ref[...] = (acc_scr[...] / l_scr[...]).astype(o_ref.dtype)


def _fox_prompt(qb, kb, vb, negc, tq=2048, rq=256):
    s = qb.shape[0]
    head = lambda h, i: (0, h)
    return pl.pallas_call(
        functools.partial(_fox_prompt_body, tq=tq, rq=rq),
        grid=(N_HEADS, s // tq),
        in_specs=[pl.BlockSpec((tq, HEAD_DIM), lambda h, i: (i, h)),
                  pl.BlockSpec((s, HEAD_DIM), head),
                  pl.BlockSpec((s, HEAD_DIM), head),
                  pl.BlockSpec((None, 1, s), lambda h, i: (h, 0, 0))],
        out_specs=pl.BlockSpec((tq, HEAD_DIM), lambda h, i: (i, h)),
        out_shape=jax.ShapeDtypeStruct((s, ATT_W), BF16),
        scratch_shapes=[pltpu.VMEM((tq, 1), F32), pltpu.VMEM((tq, 1), F32), pltpu.VMEM((tq, HEAD_DIM), F32)],
        compiler_params=_cparams(2),
        name="fox_prompt",
    )(qb, kb, vb, negc.reshape(N_HEADS, 1, s))


def _page_negcumsum(lf, run, lane, rowj):
    x = _lane_cumsum(lf, (16, 32, 64), lane)
    t = jnp.where(lane >= LANES - N_HEADS, x, 0.0)
    for sh in (16, 32, 64):
        t = t + pltpu.roll(t, sh, axis=1)
    e = t
    for sh in (1, 2, 4, 8):
        e = e + jnp.where(rowj >= sh, pltpu.roll(e, sh, axis=0), 0.0)
    c = x + (e - t) + run
    return -c, run + e[N_HEADS - 1:N_HEADS, :]


def _fox_sample_body(pt_ref, q_ref, *refs, npg):
    del pt_ref
    k_refs, v_refs, lf_refs = refs[:npg], refs[npg:2 * npg], refs[2 * npg:3 * npg]
    kn_ref, vn_ref, lfn_ref, o_ref, m_scr, l_scr, acc_scr, run_scr = refs[3 * npg:]
    g = pl.program_id(1)
    tiles = k_refs[0].shape[0] // LANES

    @pl.when(g == 0)
    def _():
        m_scr[...] = jnp.full(m_scr.shape, NEG, F32)
        l_scr[...] = jnp.zeros(l_scr.shape, F32)
        acc_scr[...] = jnp.zeros(acc_scr.shape, F32)
        run_scr[...] = jnp.zeros(run_scr.shape, F32)

    lane = lax.broadcasted_iota(jnp.int32, (SUBLANES, LANES), 1)
    sub = lax.broadcasted_iota(jnp.int32, (SUBLANES, LANES), 0)
    head_is = [(lane & (N_HEADS - 1)) == h for h in range(N_HEADS)]
    lane16 = lax.broadcasted_iota(jnp.int32, (N_HEADS, LANES), 1)
    row16 = lax.broadcasted_iota(jnp.int32, (N_HEADS, LANES), 0)
    q = q_ref[...]

    def compact(tile):
        zc = tile[(N_HEADS - 1) * SUBLANES:, :]
        for h in range(N_HEADS - 2, -1, -1):
            zc = jnp.where(head_is[h], tile[h * SUBLANES:(h + 1) * SUBLANES, :], zc)
        return zc

    def spread(pc):
        return jnp.concatenate([jnp.where(head_is[h], pc, 0.0) for h in range(N_HEADS)], axis=0)

    def over_tokens(x, op):
        for sh in (16, 32, 64):
            x = op(x, pltpu.roll(x, sh, axis=1))
        return x

    def per_row(x):
        return jnp.concatenate([jnp.broadcast_to(x[:, h:h + 1], (SUBLANES, LANES)) for h in range(N_HEADS)], axis=0)

    def attend(zcs, vbs, m, l, acc):
        zmax = zcs[0]
        for z in zcs[1:]:
            zmax = jnp.maximum(zmax, z)
        m_new = jnp.maximum(m, over_tokens(zmax, jnp.maximum))
        alpha = jnp.exp2(m - m_new)
        per = len(zcs) // len(vbs)
        psum = None
        pv = None
        for i, vb in enumerate(vbs):
            ps = [jnp.exp2(z - m_new) for z in zcs[i * per:(i + 1) * per]]
            for pp in ps:
                psum = pp if psum is None else psum + pp
            pm = jnp.concatenate([spread(pp) for pp in ps], axis=1) if per > 1 else spread(ps[0])
            d = _dot(pm.astype(BF16), vb)
            pv = d if pv is None else pv + d
        l = alpha * l + over_tokens(psum, jnp.add)
        return m_new, l, per_row(alpha) * acc + pv

    run = run_scr[...]
    zcs = []
    for p in range(npg):
        gt = _dot_nt(q, k_refs[p][...].astype(BF16))
        negc, run = _page_negcumsum(lf_refs[p][...], run, lane16, row16)
        negc = negc * LOG2E
        zcs += [compact(gt[:, t * LANES:(t + 1) * LANES]) + negc[t:t + 1, :] for t in range(tiles)]
    m, l, acc = attend(zcs, [v_refs[p][...].astype(BF16) for p in range(npg)],
                       m_scr[...], l_scr[...], acc_scr[...])
    m_scr[...], l_scr[...], acc_scr[...], run_scr[...] = m, l, acc, run

    @pl.when(g == pl.num_programs(1) - 1)
    def _():
        cn = _lane_cumsum(lfn_ref[...], (16, 32, 64), lane16[:1]) + run
        zn = compact(_dot_nt(q, kn_ref[...])) - cn * LOG2E
        _, l2, acc2 = attend([jnp.where((lane >> 4) <= sub, zn, NEG)], [vn_ref[...]], m, l, acc)
        o_ref[...] = acc2 / per_row(l2)


def _fox_sample(page_table, qcat, ck, cv, clf, kn, vn, lfn):
    nb = qcat.shape[0]
    n_pages = page_table.shape[1]
    npg = PAGES_PER_STEP
    rows = N_HEADS * SUBLANES
    prow = ck.shape[1]

    def page(p):
        return lambda b, g, pt: (pt[b * n_pages + g * npg + p], 0, 0)

    per_b = lambda b, g, pt: (b, 0, 0)
    in_specs = ([pl.BlockSpec((None, rows, HEAD_DIM), per_b)]
                + [pl.BlockSpec((None, prow, HEAD_DIM), page(p)) for p in range(npg)]
                + [pl.BlockSpec((None, prow, HEAD_DIM), page(p)) for p in range(npg)]
                + [pl.BlockSpec((None, N_HEADS, LANES), page(p)) for p in range(npg)]
                + [pl.BlockSpec((None, rows, HEAD_DIM), per_b),
                   pl.BlockSpec((None, rows, HEAD_DIM), per_b),
                   pl.BlockSpec((None, 1, LANES), per_b)])
    grid_spec = pltpu.PrefetchScalarGridSpec(
        num_scalar_prefetch=1,
        grid=(nb, n_pages // npg),
        in_specs=in_specs,
        out_specs=pl.BlockSpec((None, rows, HEAD_DIM), per_b),
        scratch_shapes=[pltpu.VMEM((SUBLANES, LANES), F32), pltpu.VMEM((SUBLANES, LANES), F32),
                        pltpu.VMEM((rows, HEAD_DIM), F32), pltpu.VMEM((1, LANES), F32)],
    )
    return pl.pallas_call(
        functools.partial(_fox_sample_body, npg=npg),
        grid_spec=grid_spec,
        out_shape=jax.ShapeDtypeStruct((nb, rows, HEAD_DIM), F32),
        compiler_params=_cparams(2),
        name="fox_sample",
    )(page_table.reshape(-1), qcat, *([ck] * npg), *([cv] * npg), *([clf] * npg), kn, vn, lfn)


def _gmlp_body(gu_ref, gv_ref, gg_ref, w_ref, bt_ref, yb_ref, *vn_ref, blk):
    r = gv_ref.shape[0]
    vn = _rms_rows(gv_ref[...], gg_ref[...])
    if vn_ref:
        vn_ref[0][...] = vn
    vnb = vn.astype(BF16)
    rr = lax.broadcasted_iota(jnp.int32, (r, r), 0)
    cc = lax.broadcasted_iota(jnp.int32, (r, r), 1)
    sh = blk.bit_length() - 1
    keep = jnp.logical_and((rr >> sh) == (cc >> sh), (cc & (blk - 1)) <= (rr & (blk - 1)))
    for g in range(GM_GROUPS):
        sl = slice(g * GM_GROUP_W, (g + 1) * GM_GROUP_W)
        w = jnp.where(keep, w_ref[g], jnp.zeros((), BF16))
        mixed = _dot(w, vnb[:, sl]) + bt_ref[:, g:g + 1]
        yb_ref[:, sl] = (gu_ref[:, sl].astype(F32) * mixed).astype(BF16)


def _gmlp(gu, gv, gg, w, bt, rows, blk, want_vn):
    m = gu.shape[0]
    row_spec = pl.BlockSpec((rows, GM_W), lambda i: (i, 0))
    out_shape = [jax.ShapeDtypeStruct((m, GM_W), BF16)]
    out_specs = [row_spec]
    if want_vn:
        out_shape.append(jax.ShapeDtypeStruct((m, GM_W), F32))
        out_specs.append(row_spec)
    return pl.pallas_call(
        functools.partial(_gmlp_body, blk=blk),
        grid=(m // rows,),
        in_specs=[row_spec, row_spec,
                  pl.BlockSpec((1, GM_W), lambda i: (0, 0)),
                  pl.BlockSpec((GM_GROUPS, rows, rows), lambda i: (0, 0, 0)),
                  pl.BlockSpec((rows, GM_GROUPS), lambda i: (0, 0))],
        out_specs=out_specs,
        out_shape=out_shape,
        compiler_params=_cparams(1),
        name="gmlp",
    )(gu, gv, gg, w, bt)


def _merge_body(ya_ref, yb_ref, woa_ref, wob_ref, sa_ref, sb_ref, m_ref):
    a = _dot(ya_ref[...], woa_ref[...])
    b = _dot(yb_ref[...], wob_ref[...])
    m_ref[...] = (sa_ref[...].astype(F32) * a + sb_ref[...].astype(F32) * b).astype(BF16)


def _merge(ya, yb, woa, wob, sa, sb, tm, tn=MERGE_TN):
    m, ka = ya.shape
    n = woa.shape[1]
    nt = n // tn
    wtile = pl.BlockSpec((ka, tn), lambda i, j: (0, j))
    return pl.pallas_call(
        _merge_body,
        grid=(m // tm, nt),
        in_specs=[pl.BlockSpec((tm, ka), lambda i, j: (i, 0)),
                  pl.BlockSpec((tm, ka), lambda i, j: (i, 0)),
                  wtile,
                  wtile,
                  pl.BlockSpec((tm, tn), lambda i, j: (i, j)),
                  pl.BlockSpec((tm, tn), lambda i, j: (i, j))],
        out_specs=pl.BlockSpec((tm, tn), lambda i, j: (i, j)),
        out_shape=jax.ShapeDtypeStruct((m, n), BF16),
        compiler_params=_cparams(2),
        name="merge",
    )(ya, yb, woa, wob, sa, sb)


def _resid_mm_body(a_ref, w_ref, x_ref, o_ref):
    o_ref[...] = x_ref[...] + _dot(a_ref[...], w_ref[...])


def _resid_mm(a, w, x, tm, tn, name):
    m, k = a.shape
    n = w.shape[1]
    return pl.pallas_call(
        _resid_mm_body,
        grid=(m // tm, n // tn),
        in_specs=[pl.BlockSpec((tm, k), lambda i, j: (i, 0)),
                  pl.BlockSpec((k, tn), lambda i, j: (0, j)),
                  pl.BlockSpec((tm, tn), lambda i, j: (i, j))],
        out_specs=pl.BlockSpec((tm, tn), lambda i, j: (i, j)),
        out_shape=jax.ShapeDtypeStruct((m, n), F32),
        compiler_params=_cparams(2),
        name=name,
    )(a, w, x)


def _ffn_up_body(x_ref, g2_ref, wu_ref, wg_ref, cw_ref, cb_ref, *refs, seq_rows):
    i = pl.program_id(0)
    j = pl.program_id(1)
    if seq_rows is None:
        act_ref, st_ref, h_scr, carry_scr = refs
    else:
        prev_ref, act_ref, st_ref, h_scr = refs

    @pl.when(j == 0)
    def _():
        h_scr[...] = _rms_rows(x_ref[...], g2_ref[...]).astype(BF16)

    ch = h_scr.shape[0] // ROW_CHAINS
    wu = wu_ref[...]
    wg = wg_ref[...]
    tn = wu.shape[1]

    def gated(c, g):
        hc = 0.5 * c
        return ((hc + hc * jnp.tanh(hc)) * g).astype(BF16)

    def conv(x2, x1, x0):
        return cb_ref[...] + cw_ref[0:1, :] * x2 + cw_ref[1:2, :] * x1 + cw_ref[2:3, :] * x0

    if seq_rows is None:
        @pl.when(i == 0)
        def _():
            carry_scr[j] = jnp.zeros(carry_scr.shape[1:], F32)

        above = carry_scr[j]
        top = lax.broadcasted_iota(jnp.int32, (SUBLANES, tn), 0)
    else:
        t = lax.broadcasted_iota(jnp.int32, (ch, tn), 0) & (seq_rows - 1)
    for ci in range(ROW_CHAINS):
        rows = slice(ci * ch, (ci + 1) * ch)
        h = h_scr[rows, :]
        a = _dot(h, wu)
        gate = _dot(h, wg)
        if seq_rows is None:
            a1 = pltpu.roll(a, 1, axis=0)
            a2 = pltpu.roll(a, 2, axis=0)
            act_ref[rows, :] = gated(conv(a2, a1, a), gate)
            first = slice(ci * ch, ci * ch + SUBLANES)
            p1 = above[SUBLANES - 1:SUBLANES, :]
            p2 = above[SUBLANES - 2:SUBLANES - 1, :]
            t1 = jnp.where(top == 0, p1, a1[:SUBLANES])
            t2 = jnp.where(top == 0, p2, jnp.where(top == 1, p1, a2[:SUBLANES]))
            act_ref[first, :] = gated(conv(t2, t1, a[:SUBLANES]), gate[:SUBLANES])
            above = a[ch - SUBLANES:, :]
        else:
            prev = prev_ref[rows, :]
            a1 = jnp.where(t == 0, pltpu.roll(prev, ch - 1, axis=0), pltpu.roll(a, 1, axis=0))
            a2 = jnp.where(t < 2, prev, pltpu.roll(a, 2, axis=0))
            st_ref[rows, :] = a
            act_ref[rows, :] = gated(conv(a2, a1, a), gate)
    if seq_rows is None:
        carry_scr[j] = above
        st_ref[...] = above


def _ffn_up(x, g2, wu, wg, cw, cb, tm, prev=None, seq_rows=None, tn=FFN_TN):
    m, d = x.shape
    f = wu.shape[1]
    nj = f // tn
    col = lambda i, j: (0, j)
    wtile = pl.BlockSpec((d, tn), col)
    x_mode = {"pipeline_mode": pl.Buffered(1)} if tm * d * 4 > VMEM_LIMIT // 4 else {}
    in_specs = [pl.BlockSpec((tm, d), lambda i, j: (i, 0), **x_mode),
                pl.BlockSpec((1, d), lambda i, j: (0, 0)),
                wtile,
                wtile,
                pl.BlockSpec((CONV_W, tn), col),
                pl.BlockSpec((1, tn), col)]
    args = [x, g2, wu, wg, cw, cb]
    scratch = [pltpu.VMEM((tm, d), BF16)]
    if seq_rows is None:
        st_shape = jax.ShapeDtypeStruct((m // tm * SUBLANES, f), F32)
        st_spec = pl.BlockSpec((SUBLANES, tn), lambda i, j: (i, j))
        scratch.append(pltpu.VMEM((nj, SUBLANES, tn), F32))
    else:
        in_specs.append(pl.BlockSpec((tm, tn), lambda i, j: (i, j)))
        args.append(prev)
        st_shape = jax.ShapeDtypeStruct((m, f), F32)
        st_spec = pl.BlockSpec((tm, tn), lambda i, j: (i, j))
    return pl.pallas_call(
        functools.partial(_ffn_up_body, seq_rows=seq_rows),
        grid=(m // tm, nj),
        in_specs=in_specs,
        out_specs=[pl.BlockSpec((tm, tn), lambda i, j: (i, j)), st_spec],
        out_shape=[jax.ShapeDtypeStruct((m, f), BF16), st_shape],
        scratch_shapes=scratch,
        compiler_params=_cparams(2),
        name="ffn_up",
    )(*args)


def _layer_dense_tail(x, ya, yb, sa, sb, wts, tm, tm_down, prev=None, seq_rows=None):
    mrg = _merge(ya, yb, wts["oa"], wts["ob"], sa, sb, tm)
    x1 = _resid_mm(mrg, wts["out"], x, tm, MERGE_TN, "out_proj")
    act, st = _ffn_up(x1, wts["g2"], wts["up"], wts["gate"], wts["cw"], wts["cb"], tm, prev, seq_rows)
    y = _resid_mm(act, wts["down"], x1, tm_down, DOWN_TN, "ffn_down")
    return y, st


def kernel(x_prompt, x_sample, cache_k, cache_v, cache_logf, state_conv, page_table, norm1_g, w_in, b_f,
           q_norm_g, k_norm_g, gm_norm_g, w_s, b_s, w_oa, w_ob, w_out, norm2_g, w_up, w_gate, conv_w, conv_b,
           w_down):
    depth = w_in.shape[0]
    assert depth == 1
    bp, sp, d = x_prompt.shape
    assert bp == 1
    bs, ts, _ = x_sample.shape
    assert ts == SUBLANES
    f = w_up.shape[-1]
    l = 0

    wt = jnp.transpose(w_in[l]).astype(BF16)
    wr = wt[3 * ATT_W + N_HEADS:]
    bfp = jnp.pad(b_f[l], (0, LANES - N_HEADS)).reshape(1, LANES)
    g1 = norm1_g[l].reshape(1, d)
    qg = q_norm_g[l].reshape(1, HEAD_DIM)
    kg = k_norm_g[l].reshape(1, HEAD_DIM)
    gg = gm_norm_g[l].reshape(1, GM_W)
    wts = {"oa": w_oa[l].astype(BF16), "ob": w_ob[l].astype(BF16), "out": w_out[l].astype(BF16),
           "g2": norm2_g[l].reshape(1, d), "up": w_up[l].astype(BF16), "gate": w_gate[l].astype(BF16),
           "cw": conv_w[l], "cb": conv_b[l].reshape(1, f), "down": w_down[l].astype(BF16)}
    ws = w_s[l].astype(BF16)

    xp = x_prompt.reshape(sp, d)
    q, kf, kb, vf, vb, gu, gv, sa, sb, lf = _inproj(xp, g1, wt, wr, bfp, qg, kg, tm_norm=512, tm=1024)
    ya = _fox_prompt(q, kb, vb, _negcumsum(lf))
    rep = GMLP_CHUNKS_PER_STEP
    (yb,) = _gmlp(gu, gv, gg, jnp.tile(ws, (1, rep, rep)), jnp.tile(jnp.transpose(b_s[l]), (rep, 1)),
                  rep * CHUNK, CHUNK, False)
    yp, st_p = _layer_dense_tail(xp, ya, yb, sa, sb, wts, 1024, 512)

    ms = bs * ts
    xs = x_sample.reshape(ms, d)
    q, kf_s, kb_s, vf_s, vb_s, gu, gv, sa, sb, lf_s = _inproj(xs, g1, wt, wr, bfp, qg, kg, tm_norm=ms, tm=ms)
    rows = N_HEADS * ts
    qcat = jnp.transpose(q.reshape(bs, ts, N_HEADS, HEAD_DIM), (0, 2, 1, 3)).reshape(bs, rows, HEAD_DIM)
    n_pool = cache_k.shape[1]
    page_rows = cache_k.shape[2] * N_HEADS
    ck = cache_k[l].reshape(n_pool, page_rows, HEAD_DIM)
    cv = cache_v[l].reshape(n_pool, page_rows, HEAD_DIM)
    clf = cache_logf[l].reshape(n_pool, page_rows // LANES, LANES)
    o = _fox_sample(page_table, qcat, ck, cv, clf,
                    kb_s.reshape(bs, rows, HEAD_DIM), vb_s.reshape(bs, rows, HEAD_DIM),
                    lf_s[:, :N_HEADS].reshape(bs, 1, rows))
    ya = jnp.transpose(o.reshape(bs, N_HEADS, ts, HEAD_DIM), (0, 2, 1, 3)).reshape(ms, ATT_W).astype(BF16)
    w_small = jnp.tile(ws[:, :ts, :ts], (1, bs, bs))
    bt_small = jnp.tile(jnp.transpose(b_s[l][:, :ts]), (bs, 1))
    yb, vn = _gmlp(gu, gv, gg, w_small, bt_small, ms, ts, True)
    prev = jnp.pad(state_conv[l], ((0, 0), (0, ts - (CONV_W - 1)), (0, 0))).reshape(ms, f)
    ys, a_s = _layer_dense_tail(xs, ya, yb, sa, sb, wts, ms, ms, prev, ts)

    return (yp.reshape(1, sp, d),
            ys.reshape(bs, ts, d),
            kf.reshape(1, 1, sp, N_HEADS, HEAD_DIM),
            vf.reshape(1, 1, sp, N_HEADS, HEAD_DIM),
            lf[:, :N_HEADS].reshape(1, 1, sp, N_HEADS),
            st_p[st_p.shape[0] - (CONV_W - 1):].reshape(1, 1, CONV_W - 1, f),
            kf_s.reshape(1, bs, ts, N_HEADS, HEAD_DIM),
            vf_s.reshape(1, bs, ts, N_HEADS, HEAD_DIM),
            lf_s[:, :N_HEADS].reshape(1, bs, ts, N_HEADS),
            vn.reshape(1, bs, ts, GM_W),
            a_s.reshape(bs, ts, f)[:, ts - (CONV_W - 1):].reshape(1, bs, CONV_W - 1, f))
```

```python
import functools

import jax
import jax.numpy as jnp
import numpy as np
from jax import lax
from jax.experimental import pallas as pl
from jax.experimental.pallas import tpu as pltpu

F32 = jnp.float32
BF16 = jnp.bfloat16

LANES = 128
SUBLANES = 8
VMEM_LIMIT = 56 * 1024 * 1024

HEAD_DIM = 128
N_HEADS = 16
ATT_W = N_HEADS * HEAD_DIM
GM_GROUPS = 16
GM_GROUP_W = 128
GM_W = GM_GROUPS * GM_GROUP_W
CHUNK = 128
CONV_W = 3
EPS = 1e-6
NEG = -1e30
SCALE = HEAD_DIM ** -0.5
LOG2E = float(np.log2(np.e))
Q_FOLD = SCALE * LOG2E

PROJ_TN = 512
MERGE_TN = 512
FFN_TN = 256
DOWN_TN = 512
GMLP_CHUNKS_PER_STEP = 4
PAGES_PER_STEP = 8


def _cparams(n_axes):
    return pltpu.CompilerParams(dimension_semantics=("arbitrary",) * n_axes, vmem_limit_bytes=VMEM_LIMIT)


def _dot(a, b):
    return jnp.dot(a, b, preferred_element_type=F32)


def _dot_nt(a, b):
    return lax.dot_general(a, b, (((1,), (1,)), ((), ())), preferred_element_type=F32)


def _rms_rows(x, g):
    return x * lax.rsqrt(jnp.mean(x * x, axis=-1, keepdims=True) + EPS) * g


def _sigmoid(x):
    return 0.5 + 0.5 * jnp.tanh(0.5 * x)


def _gelu_tanh(x):
    return 0.5 * x * (1.0 + jnp.tanh(np.sqrt(2.0 / np.pi).astype(np.float32) * (x + 0.044715 * (x * x * x))))


def _log_sigmoid(x):
    return jnp.minimum(x, 0.0) - jnp.log(1.0 + jnp.exp(-jnp.abs(x)))


ROW_CHAINS = 1


def _rmsnorm_body(x_ref, g_ref, o_ref):
    o_ref[...] = _rms_rows(x_ref[...], g_ref[...]).astype(BF16)


def _rmsnorm(x, g, tm):
    m, d = x.shape
    return pl.pallas_call(
        _rmsnorm_body,
        grid=(m // tm,),
        in_specs=[pl.BlockSpec((tm, d), lambda i: (i, 0)), pl.BlockSpec((1, d), lambda i: (0, 0))],
        out_specs=pl.BlockSpec((tm, d), lambda i: (i, 0)),
        out_shape=jax.ShapeDtypeStruct((m, d), BF16),
        compiler_params=_cparams(1),
        name="rmsnorm",
    )(x, g)


def _sections(h_ref, w_ref, table):
    j = pl.program_id(1)
    for lo, n, epilogue in table:
        @pl.when(jnp.logical_and(j >= lo, j < lo + n))
        def _(epilogue=epilogue):
            epilogue(_dot_nt(h_ref[...], w_ref[...]))


_HEADS = [slice(h * HEAD_DIM, (h + 1) * HEAD_DIM) for h in range(PROJ_TN // HEAD_DIM)]
_SEC_QKV = {"q": (0, 4), "k": (4, 4), "v": (8, 4)}
_SEC_REST = {"u": (0, 4), "vg": (4, 4), "ga": (8, 8), "gb": (16, 8)}


def _proj_qkv_body(h_ref, w_ref, wf_ref, bf_ref, qg_ref, kg_ref, q_ref, kf_ref, kb_ref, vf_ref, vb_ref, lf_ref):
    @pl.when(pl.program_id(1) == 0)
    def _():
        lf_ref[...] = _log_sigmoid(_dot_nt(h_ref[...], wf_ref[...]) + bf_ref[...])

    def q_epi(acc):
        for sl in _HEADS:
            q_ref[:, sl] = (_rms_rows(acc[:, sl], qg_ref[...]) * Q_FOLD).astype(BF16)

    def k_epi(acc):
        for sl in _HEADS:
            y = _rms_rows(acc[:, sl], kg_ref[...])
            kf_ref[:, sl] = y
            kb_ref[:, sl] = y.astype(BF16)

    def v_epi(acc):
        vf_ref[...] = acc
        vb_ref[...] = acc.astype(BF16)

    _sections(h_ref, w_ref, [(*_SEC_QKV["q"], q_epi), (*_SEC_QKV["k"], k_epi), (*_SEC_QKV["v"], v_epi)])


def _proj_rest_body(h_ref, w_ref, gu_ref, gv_ref, sa_ref, sb_ref):
    def u_epi(acc):
        gu_ref[...] = _gelu_tanh(acc).astype(BF16)

    def vg_epi(acc):
        gv_ref[...] = _gelu_tanh(acc)

    def ga_epi(acc):
        sa_ref[...] = _sigmoid(acc).astype(BF16)

    def gb_epi(acc):
        sb_ref[...] = _sigmoid(acc).astype(BF16)

    _sections(h_ref, w_ref, [(*_SEC_REST["u"], u_epi), (*_SEC_REST["vg"], vg_epi),
                             (*_SEC_REST["ga"], ga_epi), (*_SEC_REST["gb"], gb_epi)])


def _proj_call(body, name, h, w, extra, extra_specs, secs, outs, tm, row0=0, tail_out=None):
    m, d = h.shape
    nj = sum(n for _, n in secs.values())
    if row0 % PROJ_TN == 0:
        w_spec = pl.BlockSpec((PROJ_TN, d), lambda i, j: (row0 // PROJ_TN + j, 0))
    else:
        assert row0 % (2 * SUBLANES) == 0
        w_spec = pl.BlockSpec((pl.Element(PROJ_TN), pl.Element(d)),
                              lambda i, j: (pl.multiple_of(row0 + j * PROJ_TN, 2 * SUBLANES), 0))

    def sec(sname):
        lo, n = secs[sname]
        return pl.BlockSpec((tm, PROJ_TN), lambda i, j: (i, jnp.clip(j - lo, 0, n - 1)))

    out_specs = [sec(sname) for sname, _ in outs]
    out_shape = [jax.ShapeDtypeStruct((m, secs[sname][1] * PROJ_TN), dt) for sname, dt in outs]
    if tail_out:
        out_specs.append(pl.BlockSpec((tm, LANES), lambda i, j: (i, 0)))
        out_shape.append(jax.ShapeDtypeStruct((m, LANES), F32))
    return pl.pallas_call(
        body,
        grid=(m // tm, nj),
        in_specs=[pl.BlockSpec((tm, d), lambda i, j: (i, 0)),
                  w_spec] + extra_specs,
        out_specs=out_specs,
        out_shape=out_shape,
        compiler_params=_cparams(2),
        name=name,
    )(h, w, *extra)


def _inproj(x, g1, wt, wr, bfp, qg, kg, tm_norm, tm):
    d = x.shape[1]
    h = _rmsnorm(x, g1, tm_norm)
    const = lambda i, j: (0, 0)
    f_block = sum(n for _, n in _SEC_QKV.values()) * PROJ_TN // LANES
    q, kf, kb, vf, vb, lf = _proj_call(
        _proj_qkv_body, "proj_qkv", h, wt, [wt, bfp, qg, kg],
        [pl.BlockSpec((LANES, d), lambda i, j: (f_block, 0)), pl.BlockSpec((1, LANES), const),
         pl.BlockSpec((1, HEAD_DIM), const), pl.BlockSpec((1, HEAD_DIM), const)],
        _SEC_QKV, [("q", BF16), ("k", F32), ("k", BF16), ("v", F32), ("v", BF16)], tm, tail_out=True)
    if wr is None:
        rest_w, rest_row0 = wt, f_block * LANES + N_HEADS
    else:
        rest_w, rest_row0 = wr, 0
    gu, gv, sa, sb = _proj_call(
        _proj_rest_body, "proj_rest", h, rest_w, [], [],
        _SEC_REST, [("u", BF16), ("vg", F32), ("ga", BF16), ("gb", BF16)], tm, row0=rest_row0)
    return q, kf, kb, vf, vb, gu, gv, sa, sb, lf


def _lane_cumsum(x, shifts, lane):
    for sh in shifts:
        x = x + jnp.where(lane >= sh, pltpu.roll(x, sh, axis=1), 0.0)
    return x


def _negcumsum_body(lf_ref, o_ref):
    nblk = lf_ref.shape[0] // LANES
    lane = lax.broadcasted_iota(jnp.int32, (N_HEADS, LANES), 1)

    def body(b, carry):
        off = pl.multiple_of(b * LANES, LANES)
        xt = jnp.transpose(lf_ref[pl.ds(off, LANES), :])[:N_HEADS, :]
        c = _lane_cumsum(xt, (1, 2, 4, 8, 16, 32, 64), lane) + carry
        o_ref[:, pl.ds(off, LANES)] = c * (-LOG2E)
        return c[:, LANES - 1:LANES]

    lax.fori_loop(0, nblk, body, jnp.zeros((N_HEADS, 1), F32))


def _negcumsum(lf):
    s = lf.shape[0]
    return pl.pallas_call(
        _negcumsum_body,
        out_shape=jax.ShapeDtypeStruct((N_HEADS, s), F32),
        compiler_params=pltpu.CompilerParams(vmem_limit_bytes=VMEM_LIMIT),
        name="negcumsum",
    )(lf)


def _fox_prompt_body(q_ref, k_ref, v_ref, nc_ref, o_ref, m_scr, l_scr, acc_scr, *, tq, rq):
    qi = pl.program_id(1)
    nrc = tq // rq
    m_scr[...] = jnp.full(m_scr.shape, NEG, F32)
    l_scr[...] = jnp.zeros(l_scr.shape, F32)
    acc_scr[...] = jnp.zeros(acc_scr.shape, F32)

    def update(rc, zs, vs):
        rows = slice(rc * rq, (rc + 1) * rq)
        m_old = m_scr[rows, :]
        m_new = m_old
        for z in zs:
            m_new = jnp.maximum(m_new, jnp.max(z, axis=-1, keepdims=True))
        alpha = jnp.exp2(m_old - m_new)
        l_new = alpha * l_scr[rows, :]
        acc = alpha * acc_scr[rows, :]
        for z, v in zip(zs, vs):
            pr = jnp.exp2(z - m_new)
            l_new = l_new + jnp.sum(pr, axis=-1, keepdims=True)
            acc = acc + _dot(pr.astype(BF16), v)
        l_scr[rows, :] = l_new
        acc_scr[rows, :] = acc
        m_scr[rows, :] = m_new

    def scores(rc, start, n):
        return (_dot_nt(q_ref[rc * rq:(rc + 1) * rq, :], k_ref[pl.ds(start, n), :]) + nc_ref[:, pl.ds(start, n)])

    def below_diagonal(ki, carry):
        off = pl.multiple_of(ki * tq, tq)
        v = v_ref[pl.ds(off, tq), :]
        for rc in range(nrc):
            update(rc, [scores(rc, off, tq)], [v])
        return carry

    lax.fori_loop(0, qi, below_diagonal, 0)

    off = pl.multiple_of(qi * tq, tq)
    causal = (lax.broadcasted_iota(jnp.int32, (rq, rq), 0) >= lax.broadcasted_iota(jnp.int32, (rq, rq), 1))
    for rc in range(nrc):
        n = rc * rq
        dstart = pl.multiple_of(off + n, rq)
        zs = [jnp.where(causal, scores(rc, dstart, rq), NEG)]
        vs = [v_ref[pl.ds(dstart, rq), :]]
        if n:
            zs.append(scores(rc, off, n))
            vs.append(v_ref[pl.ds(off, n), :])
        update(rc, zs, vs)
    o_ref[...] = (acc_scr[...] / l_scr[...]).astype(o_ref.dtype)


def _fox_prompt(qb, kb, vb, negc, tq=2048, rq=256):
    s = qb.shape[0]
    head = lambda h, i: (0, h)
    return pl.pallas_call(
        functools.partial(_fox_prompt_body, tq=tq, rq=rq),
        grid=(N_HEADS, s // tq),
        in_specs=[pl.BlockSpec((tq, HEAD_DIM), lambda h, i: (i, h)),
                  pl.BlockSpec((s, HEAD_DIM), head),
                  pl.BlockSpec((s, HEAD_DIM), head),
                  pl.BlockSpec((None, 1, s), lambda h, i: (h, 0, 0))],
        out_specs=pl.BlockSpec((tq, HEAD_DIM), lambda h, i: (i, h)),
        out_shape=jax.ShapeDtypeStruct((s, ATT_W), BF16),
        scratch_shapes=[pltpu.VMEM((tq, 1), F32), pltpu.VMEM((tq, 1), F32), pltpu.VMEM((tq, HEAD_DIM), F32)],
        compiler_params=_cparams(2),
        name="fox_prompt",
    )(qb, kb, vb, negc.reshape(N_HEADS, 1, s))


def _page_negcumsum(lf, run, lane, rowj):
    x = _lane_cumsum(lf, (16, 32, 64), lane)
    t = jnp.where(lane >= LANES - N_HEADS, x, 0.0)
    for sh in (16, 32, 64):
        t = t + pltpu.roll(t, sh, axis=1)
    e = t
    for sh in (1, 2, 4, 8):
        e = e + jnp.where(rowj >= sh, pltpu.roll(e, sh, axis=0), 0.0)
    c = x + (e - t) + run
    return -c, run + e[N_HEADS - 1:N_HEADS, :]


def _fox_sample_body(pt_ref, q_ref, *refs, npg):
    del pt_ref
    k_refs, v_refs, lf_refs = refs[:npg], refs[npg:2 * npg], refs[2 * npg:3 * npg]
    kn_ref, vn_ref, lfn_ref, o_ref, m_scr, l_scr, acc_scr, run_scr = refs[3 * npg:]
    g = pl.program_id(1)
    tiles = k_refs[0].shape[0] // LANES

    @pl.when(g == 0)
    def _():
        m_scr[...] = jnp.full(m_scr.shape, NEG, F32)
        l_scr[...] = jnp.zeros(l_scr.shape, F32)
        acc_scr[...] = jnp.zeros(acc_scr.shape, F32)
        run_scr[...] = jnp.zeros(run_scr.shape, F32)

    lane = lax.broadcasted_iota(jnp.int32, (SUBLANES, LANES), 1)
    sub = lax.broadcasted_iota(jnp.int32, (SUBLANES, LANES), 0)
    head_is = [(lane & (N_HEADS - 1)) == h for h in range(N_HEADS)]
    lane16 = lax.broadcasted_iota(jnp.int32, (N_HEADS, LANES), 1)
    row16 = lax.broadcasted_iota(jnp.int32, (N_HEADS, LANES), 0)
    q = q_ref[...]

    def compact(tile):
        zc = tile[(N_HEADS - 1) * SUBLANES:, :]
        for h in range(N_HEADS - 2, -1, -1):
            zc = jnp.where(head_is[h], tile[h * SUBLANES:(h + 1) * SUBLANES, :], zc)
        return zc

    def spread(pc):
        return jnp.concatenate([jnp.where(head_is[h], pc, 0.0) for h in range(N_HEADS)], axis=0)

    def over_tokens(x, op):
        for sh in (16, 32, 64):
            x = op(x, pltpu.roll(x, sh, axis=1))
        return x

    def per_row(x):
        return jnp.concatenate([jnp.broadcast_to(x[:, h:h + 1], (SUBLANES, LANES)) for h in range(N_HEADS)], axis=0)

    def attend(zcs, vbs, m, l, acc):
        zmax = zcs[0]
        for z in zcs[1:]:
            zmax = jnp.maximum(zmax, z)
        m_new = jnp.maximum(m, over_tokens(zmax, jnp.maximum))
        alpha = jnp.exp2(m - m_new)
        per = len(zcs) // len(vbs)
        psum = None
        pv = None
        for i, vb in enumerate(vbs):
            ps = [jnp.exp2(z - m_new) for z in zcs[i * per:(i + 1) * per]]
            for pp in ps:
                psum = pp if psum is None else psum + pp
            pm = jnp.concatenate([spread(pp) for pp in ps], axis=1) if per > 1 else spread(ps[0])
            d = _dot(pm.astype(BF16), vb)
            pv = d if pv is None else pv + d
        l = alpha * l + over_tokens(psum, jnp.add)
        return m_new, l, per_row(alpha) * acc + pv

    run = run_scr[...]
    zcs = []
    for p in range(npg):
        gt = _dot_nt(q, k_refs[p][...].astype(BF16))
        negc, run = _page_negcumsum(lf_refs[p][...], run, lane16, row16)
        negc = negc * LOG2E
        zcs += [compact(gt[:, t * LANES:(t + 1) * LANES]) + negc[t:t + 1, :] for t in range(tiles)]
    m, l, acc = attend(zcs, [v_refs[p][...].astype(BF16) for p in range(npg)],
                       m_scr[...], l_scr[...], acc_scr[...])
    m_scr[...], l_scr[...], acc_scr[...], run_scr[...] = m, l, acc, run

    @pl.when(g == pl.num_programs(1) - 1)
    def _():
        cn = _lane_cumsum(lfn_ref[...], (16, 32, 64), lane16[:1]) + run
        zn = compact(_dot_nt(q, kn_ref[...])) - cn * LOG2E
        _, l2, acc2 = attend([jnp.where((lane >> 4) <= sub, zn, NEG)], [vn_ref[...]], m, l, acc)
        o_ref[...] = acc2 / per_row(l2)


def _fox_sample(page_table, qcat, ck, cv, clf, kn, vn, lfn):
    nb = qcat.shape[0]
    n_pages = page_table.shape[1]
    npg = PAGES_PER_STEP
    rows = N_HEADS * SUBLANES
    prow = ck.shape[1]

    def page(p):
        return lambda b, g, pt: (pt[b * n_pages + g * npg + p], 0, 0)

    per_b = lambda b, g, pt: (b, 0, 0)
    in_specs = ([pl.BlockSpec((None, rows, HEAD_DIM), per_b)]
                + [pl.BlockSpec((None, prow, HEAD_DIM), page(p)) for p in range(npg)]
                + [pl.BlockSpec((None, prow, HEAD_DIM), page(p)) for p in range(npg)]
                + [pl.BlockSpec((None, N_HEADS, LANES), page(p)) for p in range(npg)]
                + [pl.BlockSpec((None, rows, HEAD_DIM), per_b),
                   pl.BlockSpec((None, rows, HEAD_DIM), per_b),
                   pl.BlockSpec((None, 1, LANES), per_b)])
    grid_spec = pltpu.PrefetchScalarGridSpec(
        num_scalar_prefetch=1,
        grid=(nb, n_pages // npg),
        in_specs=in_specs,
        out_specs=pl.BlockSpec((None, rows, HEAD_DIM), per_b),
        scratch_shapes=[pltpu.VMEM((SUBLANES, LANES), F32), pltpu.VMEM((SUBLANES, LANES), F32),
                        pltpu.VMEM((rows, HEAD_DIM), F32), pltpu.VMEM((1, LANES), F32)],
    )
    return pl.pallas_call(
        functools.partial(_fox_sample_body, npg=npg),
        grid_spec=grid_spec,
        out_shape=jax.ShapeDtypeStruct((nb, rows, HEAD_DIM), F32),
        compiler_params=_cparams(2),
        name="fox_sample",
    )(page_table.reshape(-1), qcat, *([ck] * npg), *([cv] * npg), *([clf] * npg), kn, vn, lfn)


def _gmlp_body(gu_ref, gv_ref, gg_ref, w_ref, bt_ref, yb_ref, *vn_ref, blk):
    r = gv_ref.shape[0]
    vn = _rms_rows(gv_ref[...], gg_ref[...])
    if vn_ref:
        vn_ref[0][...] = vn
    vnb = vn.astype(BF16)
    rr = lax.broadcasted_iota(jnp.int32, (r, r), 0)
    cc = lax.broadcasted_iota(jnp.int32, (r, r), 1)
    sh = blk.bit_length() - 1
    keep = jnp.logical_and((rr >> sh) == (cc >> sh), (cc & (blk - 1)) <= (rr & (blk - 1)))
    for g in range(GM_GROUPS):
        sl = slice(g * GM_GROUP_W, (g + 1) * GM_GROUP_W)
        w = jnp.where(keep, w_ref[g], jnp.zeros((), BF16))
        mixed = _dot(w, vnb[:, sl]) + bt_ref[:, g:g + 1]
        yb_ref[:, sl] = (gu_ref[:, sl].astype(F32) * mixed).astype(BF16)


def _gmlp(gu, gv, gg, w, bt, rows, blk, want_vn):
    m = gu.shape[0]
    row_spec = pl.BlockSpec((rows, GM_W), lambda i: (i, 0))
    out_shape = [jax.ShapeDtypeStruct((m, GM_W), BF16)]
    out_specs = [row_spec]
    if want_vn:
        out_shape.append(jax.ShapeDtypeStruct((m, GM_W), F32))
        out_specs.append(row_spec)
    return pl.pallas_call(
        functools.partial(_gmlp_body, blk=blk),
        grid=(m // rows,),
        in_specs=[row_spec, row_spec,
                  pl.BlockSpec((1, GM_W), lambda i: (0, 0)),
                  pl.BlockSpec((GM_GROUPS, rows, rows), lambda i: (0, 0, 0)),
                  pl.BlockSpec((rows, GM_GROUPS), lambda i: (0, 0))],
        out_specs=out_specs,
        out_shape=out_shape,
        compiler_params=_cparams(1),
        name="gmlp",
    )(gu, gv, gg, w, bt)


def _merge_body(ya_ref, yb_ref, woa_ref, wob_ref, sa_ref, sb_ref, m_ref):
    a = _dot(ya_ref[...], woa_ref[...])
    b = _dot(yb_ref[...], wob_ref[...])
    m_ref[...] = (sa_ref[...].astype(F32) * a + sb_ref[...].astype(F32) * b).astype(BF16)


def _merge(ya, yb, woa, wob, sa, sb, tm, tn=MERGE_TN):
    m, ka = ya.shape
    n = woa.shape[1]
    nt = n // tn
    wtile = pl.BlockSpec((ka, tn), lambda i, j: (0, j))
    return pl.pallas_call(
        _merge_body,
        grid=(m // tm, nt),
        in_specs=[pl.BlockSpec((tm, ka), lambda i, j: (i, 0)),
                  pl.BlockSpec((tm, ka), lambda i, j: (i, 0)),
                  wtile,
                  wtile,
                  pl.BlockSpec((tm, tn), lambda i, j: (i, j)),
                  pl.BlockSpec((tm, tn), lambda i, j: (i, j))],
        out_specs=pl.BlockSpec((tm, tn), lambda i, j: (i, j)),
        out_shape=jax.ShapeDtypeStruct((m, n), BF16),
        compiler_params=_cparams(2),
        name="merge",
    )(ya, yb, woa, wob, sa, sb)


def _resid_mm_body(a_ref, w_ref, x_ref, o_ref):
    o_ref[...] = x_ref[...] + _dot(a_ref[...], w_ref[...])


def _resid_mm(a, w, x, tm, tn, name):
    m, k = a.shape
    n = w.shape[1]
    return pl.pallas_call(
        _resid_mm_body,
        grid=(m // tm, n // tn),
        in_specs=[pl.BlockSpec((tm, k), lambda i, j: (i, 0)),
                  pl.BlockSpec((k, tn), lambda i, j: (0, j)),
                  pl.BlockSpec((tm, tn), lambda i, j: (i, j))],
        out_specs=pl.BlockSpec((tm, tn), lambda i, j: (i, j)),
        out_shape=jax.ShapeDtypeStruct((m, n), F32),
        compiler_params=_cparams(2),
        name=name,
    )(a, w, x)


def _ffn_up_body(x_ref, g2_ref, wu_ref, wg_ref, cw_ref, cb_ref, *refs, seq_rows):
    i = pl.program_id(0)
    j = pl.program_id(1)
    if seq_rows is None:
        act_ref, st_ref, h_scr, carry_scr = refs
    else:
        prev_ref, act_ref, st_ref, h_scr = refs

    @pl.when(j == 0)
    def _():
        h_scr[...] = _rms_rows(x_ref[...], g2_ref[...]).astype(BF16)

    ch = h_scr.shape[0] // ROW_CHAINS
    wu = wu_ref[...]
    wg = wg_ref[...]
    tn = wu.shape[1]

    def gated(c, g):
        hc = 0.5 * c
        return ((hc + hc * jnp.tanh(hc)) * g).astype(BF16)

    def conv(x2, x1, x0):
        return cb_ref[...] + cw_ref[0:1, :] * x2 + cw_ref[1:2, :] * x1 + cw_ref[2:3, :] * x0

    if seq_rows is None:
        @pl.when(i == 0)
        def _():
            carry_scr[j] = jnp.zeros(carry_scr.shape[1:], F32)

        above = carry_scr[j]
        top = lax.broadcasted_iota(jnp.int32, (SUBLANES, tn), 0)
    else:
        t = lax.broadcasted_iota(jnp.int32, (ch, tn), 0) & (seq_rows - 1)
    for ci in range(ROW_CHAINS):
        rows = slice(ci * ch, (ci + 1) * ch)
        h = h_scr[rows, :]
        a = _dot(h, wu)
        gate = _dot(h, wg)
        if seq_rows is None:
            a1 = pltpu.roll(a, 1, axis=0)
            a2 = pltpu.roll(a, 2, axis=0)
            act_ref[rows, :] = gated(conv(a2, a1, a), gate)
            first = slice(ci * ch, ci * ch + SUBLANES)
            p1 = above[SUBLANES - 1:SUBLANES, :]
            p2 = above[SUBLANES - 2:SUBLANES - 1, :]
            t1 = jnp.where(top == 0, p1, a1[:SUBLANES])
            t2 = jnp.where(top == 0, p2, jnp.where(top == 1, p1, a2[:SUBLANES]))
            act_ref[first, :] = gated(conv(t2, t1, a[:SUBLANES]), gate[:SUBLANES])
            above = a[ch - SUBLANES:, :]
        else:
            prev = prev_ref[rows, :]
            a1 = jnp.where(t == 0, pltpu.roll(prev, ch - 1, axis=0), pltpu.roll(a, 1, axis=0))
            a2 = jnp.where(t < 2, prev, pltpu.roll(a, 2, axis=0))
            st_ref[rows, :] = a
            act_ref[rows, :] = gated(conv(a2, a1, a), gate)
    if seq_rows is None:
        carry_scr[j] = above
        st_ref[...] = above


def _ffn_up(x, g2, wu, wg, cw, cb, tm, prev=None, seq_rows=None, tn=FFN_TN):
    m, d = x.shape
    f = wu.shape[1]
    nj = f // tn
    col = lambda i, j: (0, j)
    wtile = pl.BlockSpec((d, tn), col)
    x_mode = {"pipeline_mode": pl.Buffered(1)} if tm * d * 4 > VMEM_LIMIT // 4 else {}
    in_specs = [pl.BlockSpec((tm, d), lambda i, j: (i, 0), **x_mode),
                pl.BlockSpec((1, d), lambda i, j: (0, 0)),
                wtile,
                wtile,
                pl.BlockSpec((CONV_W, tn), col),
                pl.BlockSpec((1, tn), col)]
    args = [x, g2, wu, wg, cw, cb]
    scratch = [pltpu.VMEM((tm, d), BF16)]
    if seq_rows is None:
        st_shape = jax.ShapeDtypeStruct((m // tm * SUBLANES, f), F32)
        st_spec = pl.BlockSpec((SUBLANES, tn), lambda i, j: (i, j))
        scratch.append(pltpu.VMEM((nj, SUBLANES, tn), F32))
    else:
        in_specs.append(pl.BlockSpec((tm, tn), lambda i, j: (i, j)))
        args.append(prev)
        st_shape = jax.ShapeDtypeStruct((m, f), F32)
        st_spec = pl.BlockSpec((tm, tn), lambda i, j: (i, j))
    return pl.pallas_call(
        functools.partial(_ffn_up_body, seq_rows=seq_rows),
        grid=(m // tm, nj),
        in_specs=in_specs,
        out_specs=[pl.BlockSpec((tm, tn), lambda i, j: (i, j)), st_spec],
        out_shape=[jax.ShapeDtypeStruct((m, f), BF16), st_shape],
        scratch_shapes=scratch,
        compiler_params=_cparams(2),
        name="ffn_up",
    )(*args)


def _layer_dense_tail(x, ya, yb, sa, sb, wts, tm, tm_down, prev=None, seq_rows=None):
    mrg = _merge(ya, yb, wts["oa"], wts["ob"], sa, sb, tm)
    x1 = _resid_mm(mrg, wts["out"], x, tm, MERGE_TN, "out_proj")
    act, st = _ffn_up(x1, wts["g2"], wts["up"], wts["gate"], wts["cw"], wts["cb"], tm, prev, seq_rows)
    y = _resid_mm(act, wts["down"], x1, tm_down, DOWN_TN, "ffn_down")
    return y, st


def kernel(x_prompt, x_sample, cache_k, cache_v, cache_logf, state_conv, page_table, norm1_g, w_in, b_f,
           q_norm_g, k_norm_g, gm_norm_g, w_s, b_s, w_oa, w_ob, w_out, norm2_g, w_up, w_gate, conv_w, conv_b,
           w_down):
    depth = w_in.shape[0]
    assert depth == 1
    bp, sp, d = x_prompt.shape
    assert bp == 1
    bs, ts, _ = x_sample.shape
    assert ts == SUBLANES
    f = w_up.shape[-1]
    l = 0

    wt = jnp.transpose(w_in[l]).astype(BF16)
    wr = None
    bfp = jnp.pad(b_f[l], (0, LANES - N_HEADS)).reshape(1, LANES)
    g1 = norm1_g[l].reshape(1, d)
    qg = q_norm_g[l].reshape(1, HEAD_DIM)
    kg = k_norm_g[l].reshape(1, HEAD_DIM)
    gg = gm_norm_g[l].reshape(1, GM_W)
    wts = {"oa": w_oa[l].astype(BF16), "ob": w_ob[l].astype(BF16), "out": w_out[l].astype(BF16),
           "g2": norm2_g[l].reshape(1, d), "up": w_up[l].astype(BF16), "gate": w_gate[l].astype(BF16),
           "cw": conv_w[l], "cb": conv_b[l].reshape(1, f), "down": w_down[l].astype(BF16)}
    ws = w_s[l].astype(BF16)

    xp = x_prompt.reshape(sp, d)
    q, kf, kb, vf, vb, gu, gv, sa, sb, lf = _inproj(xp, g1, wt, wr, bfp, qg, kg, tm_norm=512, tm=1024)
    ya = _fox_prompt(q, kb, vb, _negcumsum(lf))
    rep = GMLP_CHUNKS_PER_STEP
    (yb,) = _gmlp(gu, gv, gg, jnp.tile(ws, (1, rep, rep)), jnp.tile(jnp.transpose(b_s[l]), (rep, 1)),
                  rep * CHUNK, CHUNK, False)
    yp, st_p = _layer_dense_tail(xp, ya, yb, sa, sb, wts, 1024, 512)

    ms = bs * ts
    xs = x_sample.reshape(ms, d)
    q, kf_s, kb_s, vf_s, vb_s, gu, gv, sa, sb, lf_s = _inproj(xs, g1, wt, wr, bfp, qg, kg, tm_norm=ms, tm=ms)
    rows = N_HEADS * ts
    qcat = jnp.transpose(q.reshape(bs, ts, N_HEADS, HEAD_DIM), (0, 2, 1, 3)).reshape(bs, rows, HEAD_DIM)
    n_pool = cache_k.shape[1]
    page_rows = cache_k.shape[2] * N_HEADS
    ck = cache_k[l].reshape(n_pool, page_rows, HEAD_DIM)
    cv = cache_v[l].reshape(n_pool, page_rows, HEAD_DIM)
    clf = cache_logf[l].reshape(n_pool, page_rows // LANES, LANES)
    o = _fox_sample(page_table, qcat, ck, cv, clf,
                    kb_s.reshape(bs, rows, HEAD_DIM), vb_s.reshape(bs, rows, HEAD_DIM),
                    lf_s[:, :N_HEADS].reshape(bs, 1, rows))
    ya = jnp.transpose(o.reshape(bs, N_HEADS, ts, HEAD_DIM), (0, 2, 1, 3)).reshape(ms, ATT_W).astype(BF16)
    w_small = jnp.tile(ws[:, :ts, :ts], (1, bs, bs))
    bt_small = jnp.tile(jnp.transpose(b_s[l][:, :ts]), (bs, 1))
    yb, vn = _gmlp(gu, gv, gg, w_small, bt_small, ms, ts, True)
    prev = jnp.pad(state_conv[l], ((0, 0), (0, ts - (CONV_W - 1)), (0, 0))).reshape(ms, f)
    ys, a_s = _layer_dense_tail(xs, ya, yb, sa, sb, wts, ms, ms, prev, ts)

    return (yp.reshape(1, sp, d),
            ys.reshape(bs, ts, d),
            kf.reshape(1, 1, sp, N_HEADS, HEAD_DIM),
            vf.reshape(1, 1, sp, N_HEADS, HEAD_DIM),
            lf[:, :N_HEADS].reshape(1, 1, sp, N_HEADS),
            st_p[st_p.shape[0] - (CONV_W - 1):].reshape(1, 1, CONV_W - 1, f),
            kf_s.reshape(1, bs, ts, N_HEADS, HEAD_DIM),
            vf_s.reshape(1, bs, ts, N_HEADS, HEAD_DIM),
            lf_s[:, :N_HEADS].reshape(1, bs, ts, N_HEADS),
            vn.reshape(1, bs, ts, GM_W),
            a_s.reshape(bs, ts, f)[:, ts - (CONV_W - 1):].reshape(1, bs, CONV_W - 1, f))
```
